```python
import math
import jax, jax.numpy as jnp
from jax import lax
import numpy as np

D_MODEL = 1024
BATCH = 8
SEQ = 4096
DEPTH = 2

CHUNK = 64
Q_BLOCK = 128
RMS_EPS = 1e-6
ROPE_BASE = 10000.0

RET_HEADS = 8
RET_QK_DIM = D_MODEL // RET_HEADS
RET_V_DIM = 2 * RET_QK_DIM
RET_QK_WIDTH = RET_HEADS * RET_QK_DIM
RET_WIDTH = RET_HEADS * RET_V_DIM
RET_IN = 2 * RET_QK_WIDTH + 2 * RET_WIDTH

MLA_HEADS = 8
MLA_NOPE = 128
MLA_ROPE = 64
MLA_V = 128
MLA_WIDTH = MLA_HEADS * MLA_V
Q_LORA = 384
KV_LORA = 256
MLA_IN = Q_LORA + MLA_WIDTH

kernel_name = 'yoco_retention_mla_sandwich'


def rmsnorm(x, g):
    xf = x.astype(jnp.float32)
    y = xf * lax.rsqrt(jnp.mean(xf * xf, axis=-1, keepdims=True) + RMS_EPS)
    return (y * g.astype(jnp.float32)).astype(x.dtype)


def rope(x, pos):
    half = x.shape[-1] // 2
    inv = ROPE_BASE ** (-jnp.arange(half, dtype=jnp.float32) / half)
    ang = pos.astype(jnp.float32)[:, None] * inv[None, :]
    cos = jnp.cos(ang)[None, :, None, :].astype(x.dtype)
    sin = jnp.sin(ang)[None, :, None, :].astype(x.dtype)
    x1, x2 = x[..., :half], x[..., half:]
    return jnp.concatenate([x1 * cos - x2 * sin, x1 * sin + x2 * cos], axis=-1)


def retention_decays(dtype):
    log_g = jnp.log(1.0 - jnp.exp2(-5.0 - jnp.arange(RET_HEADS, dtype=jnp.float32)))
    idx = jnp.arange(CHUNK, dtype=jnp.float32)
    dist = jnp.abs(idx[:, None] - idx[None, :])
    inner = jnp.exp(log_g[:, None, None] * dist[None])
    xi = jnp.exp(log_g[:, None] * (idx + 1.0)[None])
    zeta = jnp.exp(log_g[:, None] * (CHUNK - 1.0 - idx)[None])
    g_c = jnp.exp(log_g * CHUNK)
    return inner.astype(dtype), xi.astype(dtype), zeta.astype(dtype), g_c.astype(dtype)


def retention_layer(x, g_pre, w_in, gn_gain, w_out, g_post):
    b, s, _ = x.shape
    nc = s // CHUNK
    h = rmsnorm(x, g_pre)
    proj = h @ w_in
    q, k, v, gate = jnp.split(proj, [RET_QK_WIDTH, 2 * RET_QK_WIDTH, 2 * RET_QK_WIDTH + RET_WIDTH], axis=-1)
    pos = jnp.arange(s)
    q = rope(q.reshape(b, s, RET_HEADS, RET_QK_DIM), pos)
    k = rope(k.reshape(b, s, RET_HEADS, RET_QK_DIM), pos) * (RET_QK_DIM ** -0.5)
    v = v.reshape(b, s, RET_HEADS, RET_V_DIM)

    def to_chunks(t):
        return t.reshape(b, nc, CHUNK, RET_HEADS, t.shape[-1]).transpose(1, 0, 3, 2, 4)

    qc, kc, vc = to_chunks(q), to_chunks(k), to_chunks(v)
    inner_dec, xi, zeta, g_c = retention_decays(x.dtype)
    scores = jnp.einsum('nbhcd,nbhmd->nbhcm', qc, kc) * inner_dec[None, None]
    inner = jnp.einsum('nbhcm,nbhme->nbhce', scores, vc)

    def step(R, inp):
        q_i, k_i, v_i = inp
        cross = jnp.einsum('bhcd,bhde->bhce', q_i, R) * xi[None, :, :, None]
        R = R * g_c[None, :, None, None] + jnp.einsum('bhcd,bhce->bhde', k_i * zeta[None, :, :, None], v_i)
        return R, cross

    R0 = jnp.zeros((b, RET_HEADS, RET_QK_DIM, RET_V_DIM), x.dtype)
    _, cross = lax.scan(step, R0, (qc, kc, vc))
    o = (inner + cross).transpose(1, 0, 3, 2, 4).reshape(b, s, RET_HEADS, RET_V_DIM)
    of = o.astype(jnp.float32)
    mu = jnp.mean(of, axis=-1, keepdims=True)
    var = jnp.mean(jnp.square(of - mu), axis=-1, keepdims=True)
    o = ((of - mu) * lax.rsqrt(var + RMS_EPS)).reshape(b, s, RET_WIDTH) * gn_gain.astype(jnp.float32)
    y = (jax.nn.silu(gate) * o.astype(x.dtype)) @ w_out
    return x + rmsnorm(y, g_post)


def shared_latent_kv(h, g_kv, w_kv_a, g_kv_lat, w_uk, w_uv):
    b, s, _ = h.shape
    src = rmsnorm(h, g_kv)
    a = src @ w_kv_a
    c_kv, k_r = a[..., :KV_LORA], a[..., KV_LORA:]
    c_kv = rmsnorm(c_kv, g_kv_lat)
    k_nope = (c_kv @ w_uk).reshape(b, s, MLA_HEADS, MLA_NOPE)
    v = (c_kv @ w_uv).reshape(b, s, MLA_HEADS, MLA_V)
    k_rope = rope(k_r[:, :, None, :], jnp.arange(s))[:, :, 0, :]
    return k_nope, k_rope, v


def mla_attention(q_nope, q_rope, k_nope, k_rope, v):
    b, s = q_nope.shape[:2]
    nb = s // Q_BLOCK
    qn = q_nope.reshape(b, nb, Q_BLOCK, MLA_HEADS, MLA_NOPE).transpose(1, 0, 2, 3, 4)
    qr = q_rope.reshape(b, nb, Q_BLOCK, MLA_HEADS, MLA_ROPE).transpose(1, 0, 2, 3, 4)
    k_chunk = jnp.arange(s) // CHUNK
    scale = (MLA_NOPE + MLA_ROPE) ** -0.5

    def block(args):
        qn_i, qr_i, i = args
        sc = jnp.einsum('bqhd,bkhd->bhqk', qn_i, k_nope) + jnp.einsum('bqhr,bkr->bhqk', qr_i, k_rope)
        sc = sc.astype(jnp.float32) * scale
        q_chunk = (i * Q_BLOCK + jnp.arange(Q_BLOCK)) // CHUNK
        mask = k_chunk[None, :] <= q_chunk[:, None]
        sc = jnp.where(mask[None, None], sc, -jnp.inf)
        p = jax.nn.softmax(sc, axis=-1).astype(v.dtype)
        return jnp.einsum('bhqk,bkhd->bqhd', p, v)

    out = lax.map(block, (qn, qr, jnp.arange(nb)))
    return out.transpose(1, 0, 2, 3, 4).reshape(b, s, MLA_WIDTH)


def mla_layer(x, g_pre, w_in, g_q_lat, w_uq, w_out, g_post, k_nope, k_rope, v):
    b, s, _ = x.shape
    h = rmsnorm(x, g_pre)
    proj = h @ w_in
    c_q, gate = proj[..., :Q_LORA], proj[..., Q_LORA:]
    q = (rmsnorm(c_q, g_q_lat) @ w_uq).reshape(b, s, MLA_HEADS, MLA_NOPE + MLA_ROPE)
    q_nope = q[..., :MLA_NOPE]
    q_rope = rope(q[..., MLA_NOPE:], jnp.arange(s))
    o = mla_attention(q_nope, q_rope, k_nope, k_rope, v)
    y = (jax.nn.silu(gate) * o) @ w_out
    return x + rmsnorm(y, g_post)


def setup_inputs(seed: int = 0) -> dict:
    key = jax.random.key(seed)
    ks = jax.random.split(key, 20)
    n_a = DEPTH // 2
    n_b = DEPTH - n_a
    f32 = jnp.float32

    def w(k, shape, fan_in):
        return jax.random.normal(k, shape, f32) * (fan_in ** -0.5)

    def gain(k, shape):
        return 1.0 + 0.02 * jax.random.normal(k, shape, f32)

    return {
        'x': jax.random.normal(ks[0], (BATCH, SEQ, D_MODEL), f32),
        'g_pre_a': gain(ks[1], (n_a, D_MODEL)),
        'w_in_a': w(ks[2], (n_a, D_MODEL, RET_IN), D_MODEL),
        'gn_gain_a': gain(ks[3], (n_a, RET_WIDTH)),
        'w_out_a': w(ks[4], (n_a, RET_WIDTH, D_MODEL), RET_WIDTH),
        'g_post_a': gain(ks[5], (n_a, D_MODEL)),
        'g_kv': gain(ks[6], (D_MODEL,)),
        'w_kv_a': w(ks[7], (D_MODEL, KV_LORA + MLA_ROPE), D_MODEL),
        'g_kv_lat': gain(ks[8], (KV_LORA,)),
        'w_uk': w(ks[9], (KV_LORA, MLA_HEADS * MLA_NOPE), KV_LORA),
        'w_uv': w(ks[10], (KV_LORA, MLA_WIDTH), KV_LORA),
        'g_pre_b': gain(ks[11], (n_b, D_MODEL)),
        'w_in_b': w(ks[12], (n_b, D_MODEL, MLA_IN), D_MODEL),
        'g_q_lat': gain(ks[13], (n_b, Q_LORA)),
        'w_uq': w(ks[14], (n_b, Q_LORA, MLA_HEADS * (MLA_NOPE + MLA_ROPE)), Q_LORA),
        'w_out_b': w(ks[15], (n_b, MLA_WIDTH, D_MODEL), MLA_WIDTH),
        'g_post_b': gain(ks[16], (n_b, D_MODEL)),
    }


def reference(x, g_pre_a, w_in_a, gn_gain_a, w_out_a, g_post_a, g_kv, w_kv_a, g_kv_lat, w_uk, w_uv,
              g_pre_b, w_in_b, g_q_lat, w_uq, w_out_b, g_post_b):
    n_a = w_in_a.shape[0]
    h = x
    k_nope = k_rope = v = None
    for layer in range(DEPTH):
        if layer < n_a:
            h = retention_layer(h, g_pre_a[layer], w_in_a[layer], gn_gain_a[layer], w_out_a[layer], g_post_a[layer])
        else:
            if layer == n_a:
                k_nope, k_rope, v = shared_latent_kv(h, g_kv, w_kv_a, g_kv_lat, w_uk, w_uv)
            j = layer - n_a
            h = mla_layer(h, g_pre_b[j], w_in_b[j], g_q_lat[j], w_uq[j], w_out_b[j], g_post_b[j], k_nope, k_rope, v)
    return h
```

```python
import functools

import jax
import jax.numpy as jnp
from jax import lax
from jax.experimental import pallas as pl
from jax.experimental.pallas import tpu as pltpu

D_MODEL = 1024
CHUNK = 64
RMS_EPS = 1e-6
ROPE_BASE = 10000.0

RET_HEADS = 8
RET_QK = 128
RET_V = 256
RET_QK_WIDTH = RET_HEADS * RET_QK
RET_WIDTH = RET_HEADS * RET_V

MLA_HEADS = 8
MLA_NOPE = 128
MLA_ROPE = 64
MLA_V = 128
MLA_WIDTH = MLA_HEADS * MLA_V
Q_LORA = 384
KV_LORA = 256

LANES = 128
MLA_QK_PAD = 2 * LANES
VMEM_LIMIT_BYTES = 56 * 1024 * 1024

RET_TOKEN_BLOCK = 512
RET_SPAN = 256
PROJ_TOKEN_BLOCK = 512
ATTN_BLOCK = 512

_BF16 = jnp.bfloat16
_F32 = jnp.float32


def _single_buffered(shape, index_map):
    return pl.BlockSpec(shape, index_map, pipeline_mode=pl.Buffered(1))


def _rms_scale(x):
    return lax.rsqrt(jnp.mean(x * x, axis=-1, keepdims=True) + RMS_EPS)


def _rotate(x, cos, sin_signed):
    return x * cos + pltpu.roll(x, LANES // 2, axis=1) * sin_signed


def _dot(a, b):
    return jnp.dot(a, b, preferred_element_type=_F32)


def _dot_nt(a, b):
    return lax.dot_general(a, b, (((1,), (1,)), ((), ())), preferred_element_type=_F32)


def _dot_tn(a, b):
    return lax.dot_general(a, b, (((0,), (0,)), ((), ())), preferred_element_type=_F32)


def _retention_kernel(x_ref, gpre_ref, win_ref, cos_ref, sin_ref, dmask_ref, xi_ref, zeta_ref,
                      gspan_ref, gn_ref, wout_ref, gpost_ref, o_ref,
                      state_ref, q_s, k_s, v_s, gate_s, y_s):
    @pl.when(pl.program_id(1) == 0)
    def _():
        state_ref[...] = jnp.zeros_like(state_ref)

    x = x_ref[0]
    hb = (x * _rms_scale(x) * gpre_ref[...]).astype(_BF16)
    cos = cos_ref[...]
    sin = sin_ref[...]

    q_all = _dot(hb, win_ref[:, 0:RET_QK_WIDTH])
    for h in range(RET_HEADS):
        q_s[h] = _rotate(q_all[:, h * RET_QK:(h + 1) * RET_QK], cos, sin)
    k_all = _dot(hb, win_ref[:, RET_QK_WIDTH:2 * RET_QK_WIDTH])
    for h in range(RET_HEADS):
        k_s[h] = _rotate(k_all[:, h * RET_QK:(h + 1) * RET_QK], cos, sin)
    v_s[...] = _dot(hb, win_ref[:, 2 * RET_QK_WIDTH:2 * RET_QK_WIDTH + RET_WIDTH]).astype(_BF16)
    gate_s[...] = _dot(hb, win_ref[:, 2 * RET_QK_WIDTH + RET_WIDTH:])

    for s in range(RET_TOKEN_BLOCK // RET_SPAN):
        rows = slice(s * RET_SPAN, (s + 1) * RET_SPAN)
        for h in range(RET_HEADS):
            cols = slice(h * RET_V, (h + 1) * RET_V)
            q = q_s[h, rows, :]
            k = k_s[h, rows, :]
            v = v_s[rows, cols]
            scores = _dot_nt(q.astype(_BF16), k.astype(_BF16)) * dmask_ref[h]
            state = state_ref[h]
            o = _dot(scores.astype(_BF16), v)
            o = o + _dot((q * xi_ref[h]).astype(_BF16), state.astype(_BF16))
            state_ref[h] = state * gspan_ref[h] + _dot_tn((k * zeta_ref[h]).astype(_BF16), v)
            mu = jnp.mean(o, axis=-1, keepdims=True)
            oc = o - mu
            var = jnp.mean(oc * oc, axis=-1, keepdims=True)
            on = oc * lax.rsqrt(var + RMS_EPS) * gn_ref[:, cols]
            gate = gate_s[rows, cols]
            silu = gate / (1.0 + jnp.exp(-gate))
            y_s[rows, cols] = (silu * on).astype(_BF16)

    y = _dot(y_s[...], wout_ref[...])
    o_ref[0] = x + y * _rms_scale(y) * gpost_ref[...]


def _retention_tables():
    log_g = jnp.log(1.0 - jnp.exp2(-5.0 - jnp.arange(RET_HEADS, dtype=_F32)))
    idx = jnp.arange(RET_SPAN, dtype=_F32)
    diff = idx[:, None] - idx[None, :]
    chunk = jnp.arange(RET_SPAN) // CHUNK
    same = chunk[:, None] == chunk[None, :]
    earlier = chunk[None, :] < chunk[:, None]
    dist = jnp.where(same, jnp.abs(diff), diff)
    decay = jnp.exp(log_g[:, None, None] * dist[None])
    k_scale = RET_QK ** -0.5
    dmask = jnp.where((same | earlier)[None], decay, 0.0) * k_scale
    xi = jnp.exp(log_g[:, None] * (idx + 1.0)[None]) * k_scale
    zeta = jnp.exp(log_g[:, None] * (RET_SPAN - 1.0 - idx)[None])
    g_span = jnp.exp(log_g * RET_SPAN)
    xi = jnp.broadcast_to(xi[:, :, None], (RET_HEADS, RET_SPAN, RET_QK))
    zeta = jnp.broadcast_to(zeta[:, :, None], (RET_HEADS, RET_SPAN, RET_QK))
    g_span = jnp.broadcast_to(g_span[:, None, None], (RET_HEADS, 1, RET_V))
    return dmask.astype(_F32), xi.astype(_F32), zeta.astype(_F32), g_span.astype(_F32)


def _rope_angles(seq, half):
    inv = ROPE_BASE ** (-jnp.arange(half, dtype=_F32) / half)
    ang = jnp.arange(seq, dtype=_F32)[:, None] * inv[None, :]
    return jnp.cos(ang), jnp.sin(ang)


def _retention_layer(x, g_pre, w_in, gn_gain, w_out, g_post):
    b, s, d = x.shape
    tb = RET_TOKEN_BLOCK
    cos, sin = _rope_angles(s, RET_QK // 2)
    cos_t = jnp.concatenate([cos, cos], axis=-1)
    sin_t = jnp.concatenate([-sin, sin], axis=-1)
    dmask, xi, zeta, g_span = _retention_tables()
    const2 = lambda i, j: (0, 0)
    const3 = lambda i, j: (0, 0, 0)
    return pl.pallas_call(
        _retention_kernel,
        out_shape=jax.ShapeDtypeStruct((b, s, d), _F32),
        grid=(b, s // tb),
        in_specs=[
            pl.BlockSpec((1, tb, d), lambda i, j: (i, j, 0)),
            _single_buffered((1, d), const2),
            _single_buffered(w_in.shape, const2),
            pl.BlockSpec((tb, RET_QK), lambda i, j: (j, 0)),
            pl.BlockSpec((tb, RET_QK), lambda i, j: (j, 0)),
            _single_buffered(dmask.shape, const3),
            _single_buffered(xi.shape, const3),
            _single_buffered(zeta.shape, const3),
            _single_buffered(g_span.shape, const3),
            _single_buffered((1, RET_WIDTH), const2),
            _single_buffered(w_out.shape, const2),
            _single_buffered((1, d), const2),
        ],
        out_specs=pl.BlockSpec((1, tb, d), lambda i, j: (i, j, 0)),
        scratch_shapes=[
            pltpu.VMEM((RET_HEADS, RET_QK, RET_V), _F32),
            pltpu.VMEM((RET_HEADS, tb, RET_QK), _F32),
            pltpu.VMEM((RET_HEADS, tb, RET_QK), _F32),
            pltpu.VMEM((tb, RET_WIDTH), _BF16),
            pltpu.VMEM((tb, RET_WIDTH), _F32),
            pltpu.VMEM((tb, RET_WIDTH), _BF16),
        ],
        compiler_params=pltpu.CompilerParams(
            dimension_semantics=("parallel", "arbitrary"),
            vmem_limit_bytes=VMEM_LIMIT_BYTES),
        name="retention_layer",
    )(x, g_pre.reshape(1, d), w_in.astype(_BF16), cos_t, sin_t, dmask, xi, zeta, g_span,
      gn_gain.reshape(1, RET_WIDTH), w_out.astype(_BF16), g_post.reshape(1, d))


def _projection_kernel(h_ref, gkv_ref, wkva_ref, gkvlat_ref, wuk_ref, wuv_ref, gpre_ref, winb_ref,
                       gqlat_ref, wuq_ref, cos_ref, sin_ref,
                       q_ref, kn_ref, kr_ref, v_ref, sg_ref):
    x = h_ref[...]
    xn = x * _rms_scale(x)
    cos = cos_ref[...]
    sin = sin_ref[...]

    a = _dot((xn * gkv_ref[...]).astype(_BF16), wkva_ref[...])
    c_kv = a[:, :KV_LORA]
    kr_ref[...] = _rotate(a[:, KV_LORA:], cos, sin).astype(_BF16)
    c_kv = (c_kv * _rms_scale(c_kv) * gkvlat_ref[...]).astype(_BF16)
    kn_ref[...] = _dot(c_kv, wuk_ref[...]).astype(_BF16)
    v_ref[...] = _dot(c_kv, wuv_ref[...]).astype(_BF16)

    proj = _dot((xn * gpre_ref[...]).astype(_BF16), winb_ref[...])
    gate = proj[:, Q_LORA:]
    sg_ref[...] = (gate / (1.0 + jnp.exp(-gate))).astype(_BF16)
    c_q = proj[:, :Q_LORA]
    c_q = (c_q * _rms_scale(c_q) * gqlat_ref[...]).astype(_BF16)
    q = _dot(c_q, wuq_ref[...])
    scale = (MLA_NOPE + MLA_ROPE) ** -0.5
    for h in range(MLA_HEADS):
        lo = h * MLA_QK_PAD
        q_ref[:, lo:lo + LANES] = (q[:, lo:lo + LANES] * scale).astype(_BF16)
        q_ref[:, lo + LANES:lo + 2 * LANES] = (
            _rotate(q[:, lo + LANES:lo + 2 * LANES], cos, sin) * scale).astype(_BF16)


def _spread_rope_columns(w):
    half = MLA_ROPE // 2
    z = jnp.zeros(w.shape[:-1] + (LANES // 2 - half,), w.dtype)
    return jnp.concatenate([w[..., :half], z, w[..., half:], z], axis=-1)


def _projections(h, g_kv, w_kv_a, g_kv_lat, w_uk, w_uv, g_pre, w_in, g_q_lat, w_uq):
    b, s, d = h.shape
    t = b * s
    tb = PROJ_TOKEN_BLOCK
    half = MLA_ROPE // 2
    cos, sin = _rope_angles(s, half)
    ones = jnp.ones((s, LANES // 2 - half), _F32)
    zeros = jnp.zeros((s, LANES // 2 - half), _F32)
    cos_t = jnp.concatenate([cos, ones, cos, ones], axis=-1)
    sin_t = jnp.concatenate([-sin, zeros, sin, zeros], axis=-1)

    wkva = jnp.concatenate([w_kv_a[:, :KV_LORA], _spread_rope_columns(w_kv_a[:, KV_LORA:])], axis=-1)
    wuq = w_uq.reshape(Q_LORA, MLA_HEADS, MLA_NOPE + MLA_ROPE)
    wuq = jnp.concatenate([wuq[..., :MLA_NOPE], _spread_rope_columns(wuq[..., MLA_NOPE:])], axis=-1)
    wuq = wuq.reshape(Q_LORA, MLA_HEADS * MLA_QK_PAD)

    const = lambda i: (0, 0)
    tok = lambda i: (i, 0)
    pos = lambda i: (i % (s // tb), 0)
    weights = [
        g_kv.reshape(1, d), wkva.astype(_BF16), g_kv_lat.reshape(1, KV_LORA), w_uk.astype(_BF16),
        w_uv.astype(_BF16), g_pre.reshape(1, d), w_in.astype(_BF16), g_q_lat.reshape(1, Q_LORA),
        wuq.astype(_BF16),
    ]
    out_widths = (MLA_HEADS * MLA_QK_PAD, MLA_HEADS * MLA_NOPE, LANES, MLA_WIDTH, MLA_WIDTH)
    outs = pl.pallas_call(
        _projection_kernel,
        out_shape=[jax.ShapeDtypeStruct((t, w), _BF16) for w in out_widths],
        grid=(t // tb,),
        in_specs=[pl.BlockSpec((tb, d), tok)]
        + [_single_buffered(w.shape, const) for w in weights]
        + [pl.BlockSpec((tb, LANES), pos), pl.BlockSpec((tb, LANES), pos)],
        out_specs=[pl.BlockSpec((tb, w), tok) for w in out_widths],
        compiler_params=pltpu.CompilerParams(
            dimension_semantics=("parallel",),
            vmem_limit_bytes=VMEM_LIMIT_BYTES),
        name="latent_projections",
    )(h.reshape(t, d), *weights, cos_t, sin_t)
    return [o.reshape(b, s, -1) for o in outs]


def _attention_kernel(q_ref, kn_ref, kr_ref, v_ref, sg_ref, h_ref, wout_ref, gpost_ref, o_ref,
                      m_s, l_s, acc_s, y_s):
    blk = ATTN_BLOCK
    qi = pl.program_id(1)
    row_chunk = lax.broadcasted_iota(jnp.int32, (blk, blk), 0) // CHUNK
    col_chunk = lax.broadcasted_iota(jnp.int32, (blk, blk), 1) // CHUNK
    visible = col_chunk <= row_chunk

    for h in range(MLA_HEADS):
        q = q_ref[0, :, h * MLA_QK_PAD:(h + 1) * MLA_QK_PAD]
        head = slice(h * MLA_V, (h + 1) * MLA_V)
        m_s[...] = jnp.full_like(m_s, -jnp.inf)
        l_s[...] = jnp.zeros_like(l_s)
        acc_s[...] = jnp.zeros_like(acc_s)

        def step(ki, diagonal):
            rows = pl.ds(pl.multiple_of(ki * blk, blk), blk)
            k = jnp.concatenate([kn_ref[0, rows, head], kr_ref[0, rows, :]], axis=-1)
            s = _dot_nt(q, k)
            if diagonal:
                s = jnp.where(visible, s, -jnp.inf)
            m_prev = m_s[...]
            m_new = jnp.maximum(m_prev, jnp.max(s, axis=-1, keepdims=True))
            p = jnp.exp(s - m_new)
            alpha = jnp.exp(m_prev - m_new)
            l_s[...] = alpha * l_s[...] + jnp.sum(p, axis=-1, keepdims=True)
            acc_s[...] = alpha * acc_s[...] + _dot(p.astype(_BF16), v_ref[0, rows, head])
            m_s[...] = m_new

        def body(ki, carry):
            step(ki, diagonal=False)
            return carry

        lax.fori_loop(0, qi, body, 0)
        step(qi, diagonal=True)
        o = acc_s[...] / l_s[...]
        y_s[:, head] = (sg_ref[0, :, head].astype(_F32) * o).astype(_BF16)

    y = _dot(y_s[...], wout_ref[...])
    o_ref[0] = h_ref[0] + y * _rms_scale(y) * gpost_ref[...]


def _attention_layer(h, q, kn, kr, v, sg, w_out, g_post):
    b, s, d = h.shape
    blk = ATTN_BLOCK
    whole = lambda i, j: (i, 0, 0)
    qblk = lambda i, j: (i, j, 0)
    const = lambda i, j: (0, 0)
    return pl.pallas_call(
        _attention_kernel,
        out_shape=jax.ShapeDtypeStruct((b, s, d), _F32),
        grid=(b, s // blk),
        in_specs=[
            pl.BlockSpec((1, blk, q.shape[-1]), qblk),
            _single_buffered((1, s, kn.shape[-1]), whole),
            _single_buffered((1, s, kr.shape[-1]), whole),
            _single_buffered((1, s, v.shape[-1]), whole),
            pl.BlockSpec((1, blk, MLA_WIDTH), qblk),
            pl.BlockSpec((1, blk, d), qblk),
            _single_buffered(w_out.shape, const),
            _single_buffered((1, d), const),
        ],
        out_specs=pl.BlockSpec((1, blk, d), qblk),
        scratch_shapes=[
            pltpu.VMEM((blk, 1), _F32),
            pltpu.VMEM((blk, 1), _F32),
            pltpu.VMEM((blk, MLA_V), _F32),
            pltpu.VMEM((blk, MLA_WIDTH), _BF16),
        ],
        compiler_params=pltpu.CompilerParams(
            dimension_semantics=("parallel", "arbitrary"),
            vmem_limit_bytes=VMEM_LIMIT_BYTES),
        name="latent_attention",
    )(q, kn, kr, v, sg, h, w_out.astype(_BF16), g_post.reshape(1, d))


@jax.jit
def kernel(x, g_pre_a, w_in_a, gn_gain_a, w_out_a, g_post_a, g_kv, w_kv_a, g_kv_lat, w_uk, w_uv,
           g_pre_b, w_in_b, g_q_lat, w_uq, w_out_b, g_post_b):
    assert w_in_a.shape[0] == 1 and w_in_b.shape[0] == 1
    h = _retention_layer(x, g_pre_a[0], w_in_a[0], gn_gain_a[0], w_out_a[0], g_post_a[0])
    q, kn, kr, v, sg = _projections(h, g_kv, w_kv_a, g_kv_lat, w_uk, w_uv,
                                    g_pre_b[0], w_in_b[0], g_q_lat[0], w_uq[0])
    return _attention_layer(h, q, kn, kr, v, sg, w_out_b[0], g_post_b[0])
```

```python
import functools

import jax
import jax.numpy as jnp
from jax import lax
from jax.experimental import pallas as pl
from jax.experimental.pallas import tpu as pltpu

D_MODEL = 1024
CHUNK = 64
RMS_EPS = 1e-6
ROPE_BASE = 10000.0

RET_HEADS = 8
RET_QK = 128
RET_V = 256
RET_QK_WIDTH = RET_HEADS * RET_QK
RET_WIDTH = RET_HEADS * RET_V

MLA_HEADS = 8
MLA_NOPE = 128
MLA_ROPE = 64
MLA_V = 128
MLA_WIDTH = MLA_HEADS * MLA_V
Q_LORA = 384
KV_LORA = 256

LANES = 128
MLA_QK_PAD = 2 * LANES
VMEM_LIMIT_BYTES = 56 * 1024 * 1024

RET_TOKEN_BLOCK = 512
RET_SPAN = 256
PROJ_TOKEN_BLOCK = 512
ATTN_BLOCK = PROJ_TOKEN_BLOCK
HEAD_GROUP = 8
ONES_ROWS = 16

LOG2_E = 1.4426950408889634

_BF16 = jnp.bfloat16
_F32 = jnp.float32


def _single_buffered(shape, index_map):
    return pl.BlockSpec(shape, index_map, pipeline_mode=pl.Buffered(1))


def _rms_scale(x):
    return lax.rsqrt(jnp.mean(x * x, axis=-1, keepdims=True) + RMS_EPS)


def _rotate(x, cos, sin_signed):
    return x * cos + pltpu.roll(x, LANES // 2, axis=1) * sin_signed


def _dot(a, b):
    return jnp.dot(a, b, preferred_element_type=_F32)


def _dot_nt(a, b):
    return lax.dot_general(a, b, (((1,), (1,)), ((), ())), preferred_element_type=_F32)


def _dot_tn(a, b):
    return lax.dot_general(a, b, (((0,), (0,)), ((), ())), preferred_element_type=_F32)


def _retention_kernel(x_ref, gpre_ref, win_ref, cos_ref, sin_ref, dmask_ref, xi_ref, zeta_ref,
                      gspan_ref, gn_ref, wout_ref, gpost_ref, o_ref,
                      state_ref, q_s, k_s, v_s, gate_s, y_s):
    @pl.when(pl.program_id(1) == 0)
    def _():
        state_ref[...] = jnp.zeros_like(state_ref)

    x = x_ref[0]
    hb = (x * _rms_scale(x) * gpre_ref[...]).astype(_BF16)
    cos = cos_ref[...]
    sin = sin_ref[...]

    q_all = _dot(hb, win_ref[:, 0:RET_QK_WIDTH])
    for h in range(RET_HEADS):
        q_s[h] = _rotate(q_all[:, h * RET_QK:(h + 1) * RET_QK], cos, sin)
    k_all = _dot(hb, win_ref[:, RET_QK_WIDTH:2 * RET_QK_WIDTH])
    for h in range(RET_HEADS):
        k_s[h] = _rotate(k_all[:, h * RET_QK:(h + 1) * RET_QK], cos, sin)
    v_s[...] = _dot(hb, win_ref[:, 2 * RET_QK_WIDTH:2 * RET_QK_WIDTH + RET_WIDTH]).astype(_BF16)
    gate_s[...] = _dot(hb, win_ref[:, 2 * RET_QK_WIDTH + RET_WIDTH:])

    for s in range(RET_TOKEN_BLOCK // RET_SPAN):
        rows = slice(s * RET_SPAN, (s + 1) * RET_SPAN)
        for h in range(RET_HEADS):
            cols = slice(h * RET_V, (h + 1) * RET_V)
            q = q_s[h, rows, :]
            k = k_s[h, rows, :]
            v = v_s[rows, cols]
            scores = _dot_nt(q.astype(_BF16), k.astype(_BF16)) * dmask_ref[h]
            state = state_ref[h]
            o = _dot(scores.astype(_BF16), v)
            o = o + _dot((q * xi_ref[h]).astype(_BF16), state.astype(_BF16))
            state_ref[h] = state * gspan_ref[h] + _dot_tn((k * zeta_ref[h]).astype(_BF16), v)
            mu = jnp.mean(o, axis=-1, keepdims=True)
            oc = o - mu
            var = jnp.mean(oc * oc, axis=-1, keepdims=True)
            on = oc * lax.rsqrt(var + RMS_EPS) * gn_ref[:, cols]
            gate = gate_s[rows, cols]
            silu = gate / (1.0 + jnp.exp(-gate))
            y_s[rows, cols] = (silu * on).astype(_BF16)

    y = _dot(y_s[...], wout_ref[...])
    o_ref[0] = x + y * _rms_scale(y) * gpost_ref[...]


def _retention_tables():
    log_g = jnp.log(1.0 - jnp.exp2(-5.0 - jnp.arange(RET_HEADS, dtype=_F32)))
    idx = jnp.arange(RET_SPAN, dtype=_F32)
    diff = idx[:, None] - idx[None, :]
    chunk = jnp.arange(RET_SPAN) // CHUNK
    same = chunk[:, None] == chunk[None, :]
    earlier = chunk[None, :] < chunk[:, None]
    dist = jnp.where(same, jnp.abs(diff), diff)
    decay = jnp.exp(log_g[:, None, None] * dist[None])
    k_scale = RET_QK ** -0.5
    dmask = jnp.where((same | earlier)[None], decay, 0.0) * k_scale
    xi = jnp.exp(log_g[:, None] * (idx + 1.0)[None]) * k_scale
    zeta = jnp.exp(log_g[:, None] * (RET_SPAN - 1.0 - idx)[None])
    g_span = jnp.exp(log_g * RET_SPAN)
    xi = jnp.broadcast_to(xi[:, :, None], (RET_HEADS, RET_SPAN, RET_QK))
    zeta = jnp.broadcast_to(zeta[:, :, None], (RET_HEADS, RET_SPAN, RET_QK))
    g_span = jnp.broadcast_to(g_span[:, None, None], (RET_HEADS, 1, RET_V))
    return dmask.astype(_F32), xi.astype(_F32), zeta.astype(_F32), g_span.astype(_F32)


def _rope_angles(seq, half):
    inv = ROPE_BASE ** (-jnp.arange(half, dtype=_F32) / half)
    ang = jnp.arange(seq, dtype=_F32)[:, None] * inv[None, :]
    return jnp.cos(ang), jnp.sin(ang)


def _retention_layer(x, g_pre, w_in, gn_gain, w_out, g_post):
    b, s, d = x.shape
    tb = RET_TOKEN_BLOCK
    cos, sin = _rope_angles(s, RET_QK // 2)
    cos_t = jnp.concatenate([cos, cos], axis=-1)
    sin_t = jnp.concatenate([-sin, sin], axis=-1)
    dmask, xi, zeta, g_span = _retention_tables()
    const2 = lambda i, j: (0, 0)
    const3 = lambda i, j: (0, 0, 0)
    return pl.pallas_call(
        _retention_kernel,
        out_shape=jax.ShapeDtypeStruct((b, s, d), _F32),
        grid=(b, s // tb),
        in_specs=[
            pl.BlockSpec((1, tb, d), lambda i, j: (i, j, 0)),
            _single_buffered((1, d), const2),
            _single_buffered(w_in.shape, const2),
            pl.BlockSpec((tb, RET_QK), lambda i, j: (j, 0)),
            pl.BlockSpec((tb, RET_QK), lambda i, j: (j, 0)),
            _single_buffered(dmask.shape, const3),
            _single_buffered(xi.shape, const3),
            _single_buffered(zeta.shape, const3),
            _single_buffered(g_span.shape, const3),
            _single_buffered((1, RET_WIDTH), const2),
            _single_buffered(w_out.shape, const2),
            _single_buffered((1, d), const2),
        ],
        out_specs=pl.BlockSpec((1, tb, d), lambda i, j: (i, j, 0)),
        scratch_shapes=[
            pltpu.VMEM((RET_HEADS, RET_QK, RET_V), _F32),
            pltpu.VMEM((RET_HEADS, tb, RET_QK), _F32),
            pltpu.VMEM((RET_HEADS, tb, RET_QK), _F32),
            pltpu.VMEM((tb, RET_WIDTH), _BF16),
            pltpu.VMEM((tb, RET_WIDTH), _F32),
            pltpu.VMEM((tb, RET_WIDTH), _BF16),
        ],
        compiler_params=pltpu.CompilerParams(
            dimension_semantics=("parallel", "arbitrary"),
            vmem_limit_bytes=VMEM_LIMIT_BYTES),
        name="retention_layer",
    )(x, g_pre.reshape(1, d), w_in.astype(_BF16), cos_t, sin_t, dmask, xi, zeta, g_span,
      gn_gain.reshape(1, RET_WIDTH), w_out.astype(_BF16), g_post.reshape(1, d))


def _projection_kernel(h_ref, gkv_ref, wkva_ref, gkvlat_ref, wuk_ref, wuvt_ref, gpre_ref, winb_ref,
                       gqlat_ref, wuq_ref, cos_ref, sin_ref,
                       q_ref, kn_ref, kr_ref, vt_ref, sg_ref):
    x = h_ref[...]
    xn = x * _rms_scale(x)
    cos = cos_ref[...]
    sin = sin_ref[...]

    a = _dot((xn * gkv_ref[...]).astype(_BF16), wkva_ref[...])
    c_kv = a[:, :KV_LORA]
    kr_ref[...] = _rotate(a[:, KV_LORA:], cos, sin).astype(_BF16)
    c_kv = (c_kv * _rms_scale(c_kv) * gkvlat_ref[...]).astype(_BF16)
    kn_ref[...] = _dot(c_kv, wuk_ref[...]).astype(_BF16)
    vt_ref[0] = _dot_nt(wuvt_ref[...], c_kv).astype(_BF16)

    proj = _dot((xn * gpre_ref[...]).astype(_BF16), winb_ref[...])
    gate = proj[:, Q_LORA:]
    sg_ref[...] = (gate / (1.0 + jnp.exp(-gate))).astype(_BF16)
    c_q = proj[:, :Q_LORA]
    c_q = (c_q * _rms_scale(c_q) * gqlat_ref[...]).astype(_BF16)
    q = _dot(c_q, wuq_ref[...])
    scale = (MLA_NOPE + MLA_ROPE) ** -0.5 * LOG2_E
    for h in range(MLA_HEADS):
        lo = h * MLA_QK_PAD
        q_ref[:, lo:lo + LANES] = (q[:, lo:lo + LANES] * scale).astype(_BF16)
        q_ref[:, lo + LANES:lo + 2 * LANES] = (
            _rotate(q[:, lo + LANES:lo + 2 * LANES], cos, sin) * scale).astype(_BF16)


def _spread_rope_columns(w):
    half = MLA_ROPE // 2
    z = jnp.zeros(w.shape[:-1] + (LANES // 2 - half,), w.dtype)
    return jnp.concatenate([w[..., :half], z, w[..., half:], z], axis=-1)


def _projections(h, g_kv, w_kv_a, g_kv_lat, w_uk, w_uv, g_pre, w_in, g_q_lat, w_uq):
    b, s, d = h.shape
    t = b * s
    tb = PROJ_TOKEN_BLOCK
    half = MLA_ROPE // 2
    cos, sin = _rope_angles(s, half)
    ones = jnp.ones((s, LANES // 2 - half), _F32)
    zeros = jnp.zeros((s, LANES // 2 - half), _F32)
    cos_t = jnp.concatenate([cos, ones, cos, ones], axis=-1)
    sin_t = jnp.concatenate([-sin, zeros, sin, zeros], axis=-1)

    wkva = jnp.concatenate([w_kv_a[:, :KV_LORA], _spread_rope_columns(w_kv_a[:, KV_LORA:])], axis=-1)
    wuq = w_uq.reshape(Q_LORA, MLA_HEADS, MLA_NOPE + MLA_ROPE)
    wuq = jnp.concatenate([wuq[..., :MLA_NOPE], _spread_rope_columns(wuq[..., MLA_NOPE:])], axis=-1)
    wuq = wuq.reshape(Q_LORA, MLA_HEADS * MLA_QK_PAD)

    const = lambda i: (0, 0)
    tok = lambda i: (i, 0)
    pos = lambda i: (i % (s // tb), 0)
    weights = [
        g_kv.reshape(1, d), wkva.astype(_BF16), g_kv_lat.reshape(1, KV_LORA), w_uk.astype(_BF16),
        w_uv.T.astype(_BF16), g_pre.reshape(1, d), w_in.astype(_BF16), g_q_lat.reshape(1, Q_LORA),
        wuq.astype(_BF16),
    ]

    def token_major(width):
        return jax.ShapeDtypeStruct((t, width), _BF16), pl.BlockSpec((tb, width), tok)

    vt_out = (jax.ShapeDtypeStruct((t // tb, MLA_WIDTH, tb), _BF16),
              pl.BlockSpec((1, MLA_WIDTH, tb), lambda i: (i, 0, 0)))
    outs = [token_major(MLA_HEADS * MLA_QK_PAD), token_major(MLA_HEADS * MLA_NOPE),
            token_major(LANES), vt_out, token_major(MLA_WIDTH)]
    q, kn, kr, vt, sg = pl.pallas_call(
        _projection_kernel,
        out_shape=[o[0] for o in outs],
        grid=(t // tb,),
        in_specs=[pl.BlockSpec((tb, d), tok)]
        + [_single_buffered(w.shape, const) for w in weights]
        + [pl.BlockSpec((tb, LANES), pos), pl.BlockSpec((tb, LANES), pos)],
        out_specs=[o[1] for o in outs],
        compiler_params=pltpu.CompilerParams(
            dimension_semantics=("parallel",),
            vmem_limit_bytes=VMEM_LIMIT_BYTES),
        name="latent_projections",
    )(h.reshape(t, d), *weights, cos_t, sin_t)
    q, kn, kr, sg = [o.reshape(b, s, -1) for o in (q, kn, kr, sg)]
    return q, kn, kr, vt.reshape(b, s // tb, MLA_WIDTH, tb), sg


def _attention_kernel(q_ref, kn_ref, kr_ref, vt_ref, sg_ref, h_ref, wout_ref, gpost_ref, o_ref,
                      m_s, acc_s, y_s):
    blk = ATTN_BLOCK
    qi = pl.program_id(1)
    key_chunk = lax.broadcasted_iota(jnp.int32, (blk, blk), 0) // CHUNK
    query_chunk = lax.broadcasted_iota(jnp.int32, (blk, blk), 1) // CHUNK
    visible = key_chunk <= query_chunk
    ones = jnp.ones((ONES_ROWS, blk), _BF16)

    for h0 in range(0, MLA_HEADS, HEAD_GROUP):
        heads = range(h0, h0 + HEAD_GROUP)
        m_s[...] = jnp.full_like(m_s, -jnp.inf)
        acc_s[...] = jnp.zeros_like(acc_s)

        def step(ki, diagonal):
            rows = pl.ds(pl.multiple_of(ki * blk, blk), blk)
            k_rope = kr_ref[0, rows, :]
            scores = []
            for h in heads:
                q = q_ref[0, :, h * MLA_QK_PAD:(h + 1) * MLA_QK_PAD]
                k = jnp.concatenate([kn_ref[0, rows, h * MLA_NOPE:(h + 1) * MLA_NOPE], k_rope], axis=-1)
                scores.append(_dot_nt(k, q))
            for i, h in enumerate(heads):
                st = scores[i]
                if diagonal:
                    st = jnp.where(visible, st, -jnp.inf)
                m_prev = m_s[i]
                m_new = jnp.maximum(m_prev, jnp.max(st, axis=0, keepdims=True))
                pt = jnp.exp2(st - m_new).astype(_BF16)
                alpha = jnp.exp2(m_prev - m_new)
                vt = jnp.concatenate([vt_ref[0, ki, h * MLA_V:(h + 1) * MLA_V, :], ones], axis=0)
                acc_s[i] = alpha * acc_s[i] + _dot(vt, pt)
                m_s[i] = m_new

        def body(ki, carry):
            step(ki, diagonal=False)
            return carry

        lax.fori_loop(0, qi, body, 0)
        step(qi, diagonal=True)
        for i, h in enumerate(heads):
            head = slice(h * MLA_V, (h + 1) * MLA_V)
            o = (acc_s[i, 0:MLA_V, :] / acc_s[i, MLA_V:MLA_V + 1, :]).T
            y_s[:, head] = (sg_ref[0, :, head].astype(_F32) * o).astype(_BF16)

    y = _dot(y_s[...], wout_ref[...])
    o_ref[0] = h_ref[0] + y * _rms_scale(y) * gpost_ref[...]


def _attention_layer(h, q, kn, kr, vt, sg, w_out, g_post):
    b, s, d = h.shape
    blk = ATTN_BLOCK
    assert vt.shape == (b, s // blk, MLA_WIDTH, blk)
    whole = lambda i, j: (i, 0, 0)
    qblk = lambda i, j: (i, j, 0)
    const = lambda i, j: (0, 0)
    return pl.pallas_call(
        _attention_kernel,
        out_shape=jax.ShapeDtypeStruct((b, s, d), _F32),
        grid=(b, s // blk),
        in_specs=[
            pl.BlockSpec((1, blk, q.shape[-1]), qblk),
            _single_buffered((1, s, kn.shape[-1]), whole),
            _single_buffered((1, s, kr.shape[-1]), whole),
            _single_buffered((1,) + vt.shape[1:], lambda i, j: (i, 0, 0, 0)),
            pl.BlockSpec((1, blk, MLA_WIDTH), qblk),
            pl.BlockSpec((1, blk, d), qblk),
            _single_buffered(w_out.shape, const),
            _single_buffered((1, d), const),
        ],
        out_specs=pl.BlockSpec((1, blk, d), qblk),
        scratch_shapes=[
            pltpu.VMEM((HEAD_GROUP, 1, blk), _F32),
            pltpu.VMEM((HEAD_GROUP, MLA_V + ONES_ROWS, blk), _F32),
            pltpu.VMEM((blk, MLA_WIDTH), _BF16),
        ],
        compiler_params=pltpu.CompilerParams(
            dimension_semantics=("parallel", "arbitrary"),
            vmem_limit_bytes=VMEM_LIMIT_BYTES),
        name="latent_attention",
    )(q, kn, kr, vt, sg, h, w_out.astype(_BF16), g_post.reshape(1, d))


@jax.jit
def kernel(x, g_pre_a, w_in_a, gn_gain_a, w_out_a, g_post_a, g_kv, w_kv_a, g_kv_lat, w_uk, w_uv,
           g_pre_b, w_in_b, g_q_lat, w_uq, w_out_b, g_post_b):
    assert w_in_a.shape[0] == 1 and w_in_b.shape[0] == 1
    h = _retention_layer(x, g_pre_a[0], w_in_a[0], gn_gain_a[0], w_out_a[0], g_post_a[0])
    q, kn, kr, vt, sg = _projections(h, g_kv, w_kv_a, g_kv_lat, w_uk, w_uv,
                                     g_pre_b[0], w_in_b[0], g_q_lat[0], w_uq[0])
    return _attention_layer(h, q, kn, kr, vt, sg, w_out_b[0], g_post_b[0])
```

```python
import functools

import jax
import jax.numpy as jnp
from jax import lax
from jax.experimental import pallas as pl
from jax.experimental.pallas import tpu as pltpu

D_MODEL = 1024
CHUNK = 64
RMS_EPS = 1e-6
ROPE_BASE = 10000.0

RET_HEADS = 8
RET_QK = 128
RET_V = 256
RET_QK_WIDTH = RET_HEADS * RET_QK
RET_WIDTH = RET_HEADS * RET_V

MLA_HEADS = 8
MLA_NOPE = 128
MLA_ROPE = 64
MLA_V = 128
MLA_WIDTH = MLA_HEADS * MLA_V
Q_LORA = 384
KV_LORA = 256

LANES = 128
MLA_QK_PAD = 2 * LANES
VMEM_LIMIT_BYTES = 56 * 1024 * 1024

RET_TOKEN_BLOCK = 512
RET_SPAN = 256
PROJ_TOKEN_BLOCK = 512
ATTN_BLOCK = PROJ_TOKEN_BLOCK
HEAD_GROUP = 8
SCORE_LOOKAHEAD = 2
ONES_ROWS = 16

LOG2_E = 1.4426950408889634

_BF16 = jnp.bfloat16
_F32 = jnp.float32


def _single_buffered(shape, index_map):
    return pl.BlockSpec(shape, index_map, pipeline_mode=pl.Buffered(1))


def _rms_scale(x):
    return lax.rsqrt(jnp.mean(x * x, axis=-1, keepdims=True) + RMS_EPS)


def _rotate(x, cos, sin_signed):
    return x * cos + pltpu.roll(x, LANES // 2, axis=1) * sin_signed


def _dot(a, b):
    return jnp.dot(a, b, preferred_element_type=_F32)


def _dot_nt(a, b):
    return lax.dot_general(a, b, (((1,), (1,)), ((), ())), preferred_element_type=_F32)


def _dot_tn(a, b):
    return lax.dot_general(a, b, (((0,), (0,)), ((), ())), preferred_element_type=_F32)


def _retention_kernel(x_ref, gpre_ref, win_ref, cos_ref, sin_ref, dmask_ref, xi_ref, zeta_ref,
                      gspan_ref, gn_ref, wout_ref, gpost_ref, o_ref,
                      state_ref, q_s, k_s, v_s, gate_s, y_s):
    @pl.when(pl.program_id(1) == 0)
    def _():
        state_ref[...] = jnp.zeros_like(state_ref)

    x = x_ref[0]
    hb = (x * _rms_scale(x) * gpre_ref[...]).astype(_BF16)
    cos = cos_ref[...]
    sin = sin_ref[...]

    q_all = _dot(hb, win_ref[:, 0:RET_QK_WIDTH])
    for h in range(RET_HEADS):
        q_s[h] = _rotate(q_all[:, h * RET_QK:(h + 1) * RET_QK], cos, sin)
    k_all = _dot(hb, win_ref[:, RET_QK_WIDTH:2 * RET_QK_WIDTH])
    for h in range(RET_HEADS):
        k_s[h] = _rotate(k_all[:, h * RET_QK:(h + 1) * RET_QK], cos, sin)
    v_s[...] = _dot(hb, win_ref[:, 2 * RET_QK_WIDTH:2 * RET_QK_WIDTH + RET_WIDTH]).astype(_BF16)
    gate_s[...] = _dot(hb, win_ref[:, 2 * RET_QK_WIDTH + RET_WIDTH:])

    for s in range(RET_TOKEN_BLOCK // RET_SPAN):
        rows = slice(s * RET_SPAN, (s + 1) * RET_SPAN)
        for h in range(RET_HEADS):
            cols = slice(h * RET_V, (h + 1) * RET_V)
            q = q_s[h, rows, :]
            k = k_s[h, rows, :]
            v = v_s[rows, cols]
            scores = _dot_nt(q.astype(_BF16), k.astype(_BF16)) * dmask_ref[h]
            state = state_ref[h]
            o = _dot(scores.astype(_BF16), v)
            o = o + _dot((q * xi_ref[h]).astype(_BF16), state.astype(_BF16))
            state_ref[h] = state * gspan_ref[h] + _dot_tn((k * zeta_ref[h]).astype(_BF16), v)
            mu = jnp.mean(o, axis=-1, keepdims=True)
            oc = o - mu
            var = jnp.mean(oc * oc, axis=-1, keepdims=True)
            on = oc * lax.rsqrt(var + RMS_EPS) * gn_ref[:, cols]
            gate = gate_s[rows, cols]
            silu = gate / (1.0 + jnp.exp(-gate))
            y_s[rows, cols] = (silu * on).astype(_BF16)

    y = _dot(y_s[...], wout_ref[...])
    o_ref[0] = x + y * _rms_scale(y) * gpost_ref[...]


def _retention_tables():
    log_g = jnp.log(1.0 - jnp.exp2(-5.0 - jnp.arange(RET_HEADS, dtype=_F32)))
    idx = jnp.arange(RET_SPAN, dtype=_F32)
    diff = idx[:, None] - idx[None, :]
    chunk = jnp.arange(RET_SPAN) // CHUNK
    same = chunk[:, None] == chunk[None, :]
    earlier = chunk[None, :] < chunk[:, None]
    dist = jnp.where(same, jnp.abs(diff), diff)
    decay = jnp.exp(log_g[:, None, None] * dist[None])
    k_scale = RET_QK ** -0.5
    dmask = jnp.where((same | earlier)[None], decay, 0.0) * k_scale
    xi = jnp.exp(log_g[:, None] * (idx + 1.0)[None]) * k_scale
    zeta = jnp.exp(log_g[:, None] * (RET_SPAN - 1.0 - idx)[None])
    g_span = jnp.exp(log_g * RET_SPAN)
    xi = jnp.broadcast_to(xi[:, :, None], (RET_HEADS, RET_SPAN, RET_QK))
    zeta = jnp.broadcast_to(zeta[:, :, None], (RET_HEADS, RET_SPAN, RET_QK))
    g_span = jnp.broadcast_to(g_span[:, None, None], (RET_HEADS, 1, RET_V))
    return dmask.astype(_F32), xi.astype(_F32), zeta.astype(_F32), g_span.astype(_F32)


def _rope_angles(seq, half):
    inv = ROPE_BASE ** (-jnp.arange(half, dtype=_F32) / half)
    ang = jnp.arange(seq, dtype=_F32)[:, None] * inv[None, :]
    return jnp.cos(ang), jnp.sin(ang)


def _retention_layer(x, g_pre, w_in, gn_gain, w_out, g_post):
    b, s, d = x.shape
    tb = RET_TOKEN_BLOCK
    cos, sin = _rope_angles(s, RET_QK // 2)
    cos_t = jnp.concatenate([cos, cos], axis=-1)
    sin_t = jnp.concatenate([-sin, sin], axis=-1)
    dmask, xi, zeta, g_span = _retention_tables()
    const2 = lambda i, j: (0, 0)
    const3 = lambda i, j: (0, 0, 0)
    return pl.pallas_call(
        _retention_kernel,
        out_shape=jax.ShapeDtypeStruct((b, s, d), _F32),
        grid=(b, s // tb),
        in_specs=[
            pl.BlockSpec((1, tb, d), lambda i, j: (i, j, 0)),
            _single_buffered((1, d), const2),
            _single_buffered(w_in.shape, const2),
            pl.BlockSpec((tb, RET_QK), lambda i, j: (j, 0)),
            pl.BlockSpec((tb, RET_QK), lambda i, j: (j, 0)),
            _single_buffered(dmask.shape, const3),
            _single_buffered(xi.shape, const3),
            _single_buffered(zeta.shape, const3),
            _single_buffered(g_span.shape, const3),
            _single_buffered((1, RET_WIDTH), const2),
            _single_buffered(w_out.shape, const2),
            _single_buffered((1, d), const2),
        ],
        out_specs=pl.BlockSpec((1, tb, d), lambda i, j: (i, j, 0)),
        scratch_shapes=[
            pltpu.VMEM((RET_HEADS, RET_QK, RET_V), _F32),
            pltpu.VMEM((RET_HEADS, tb, RET_QK), _F32),
            pltpu.VMEM((RET_HEADS, tb, RET_QK), _F32),
            pltpu.VMEM((tb, RET_WIDTH), _BF16),
            pltpu.VMEM((tb, RET_WIDTH), _F32),
            pltpu.VMEM((tb, RET_WIDTH), _BF16),
        ],
        compiler_params=pltpu.CompilerParams(
            dimension_semantics=("parallel", "arbitrary"),
            vmem_limit_bytes=VMEM_LIMIT_BYTES),
        name="retention_layer",
    )(x, g_pre.reshape(1, d), w_in.astype(_BF16), cos_t, sin_t, dmask, xi, zeta, g_span,
      gn_gain.reshape(1, RET_WIDTH), w_out.astype(_BF16), g_post.reshape(1, d))


def _projection_kernel(h_ref, gkv_ref, wkva_ref, gkvlat_ref, wuk_ref, wuvt_ref, gpre_ref, winb_ref,
                       gqlat_ref, wuqt_ref, cos_ref, sin_ref, cost_ref, sint_ref,
                       qt_ref, kn_ref, kr_ref, vt_ref, sg_ref):
    x = h_ref[...]
    xn = x * _rms_scale(x)
    cos = cos_ref[...]
    sin = sin_ref[...]

    a = _dot((xn * gkv_ref[...]).astype(_BF16), wkva_ref[...])
    c_kv = a[:, :KV_LORA]
    kr_ref[...] = _rotate(a[:, KV_LORA:], cos, sin).astype(_BF16)
    c_kv = (c_kv * _rms_scale(c_kv) * gkvlat_ref[...]).astype(_BF16)
    kn_ref[...] = _dot(c_kv, wuk_ref[...]).astype(_BF16)
    vt_ref[0] = _dot_nt(wuvt_ref[...], c_kv).astype(_BF16)

    proj = _dot((xn * gpre_ref[...]).astype(_BF16), winb_ref[...])
    gate = proj[:, Q_LORA:]
    sg_ref[...] = (gate / (1.0 + jnp.exp(-gate))).astype(_BF16)
    c_q = proj[:, :Q_LORA]
    c_q = (c_q * _rms_scale(c_q) * gqlat_ref[...]).astype(_BF16)
    qt = _dot_nt(wuqt_ref[...], c_q)
    scale = (MLA_NOPE + MLA_ROPE) ** -0.5 * LOG2_E
    cos_rows = cost_ref[...]
    sin_rows = sint_ref[...]
    for h in range(MLA_HEADS):
        lo = h * MLA_QK_PAD
        qt_ref[0, lo:lo + LANES, :] = (qt[lo:lo + LANES] * scale).astype(_BF16)
        r = qt[lo + LANES:lo + 2 * LANES]
        partner = jnp.concatenate([r[LANES // 2:], r[:LANES // 2]], axis=0)
        qt_ref[0, lo + LANES:lo + 2 * LANES, :] = (
            (r * cos_rows + partner * sin_rows) * scale).astype(_BF16)


def _spread_rope_columns(w):
    half = MLA_ROPE // 2
    z = jnp.zeros(w.shape[:-1] + (LANES // 2 - half,), w.dtype)
    return jnp.concatenate([w[..., :half], z, w[..., half:], z], axis=-1)


def _projections(h, g_kv, w_kv_a, g_kv_lat, w_uk, w_uv, g_pre, w_in, g_q_lat, w_uq):
    b, s, d = h.shape
    t = b * s
    tb = PROJ_TOKEN_BLOCK
    half = MLA_ROPE // 2
    cos, sin = _rope_angles(s, half)
    ones = jnp.ones((s, LANES // 2 - half), _F32)
    zeros = jnp.zeros((s, LANES // 2 - half), _F32)
    cos_t = jnp.concatenate([cos, ones, cos, ones], axis=-1)
    sin_t = jnp.concatenate([-sin, zeros, sin, zeros], axis=-1)

    wkva = jnp.concatenate([w_kv_a[:, :KV_LORA], _spread_rope_columns(w_kv_a[:, KV_LORA:])], axis=-1)
    wuq = w_uq.reshape(Q_LORA, MLA_HEADS, MLA_NOPE + MLA_ROPE)
    wuq = jnp.concatenate([wuq[..., :MLA_NOPE], _spread_rope_columns(wuq[..., MLA_NOPE:])], axis=-1)
    wuq = wuq.reshape(Q_LORA, MLA_HEADS * MLA_QK_PAD)

    const = lambda i: (0, 0)
    tok = lambda i: (i, 0)
    pos = lambda i: (i % (s // tb), 0)
    pos_t = lambda i: (0, i % (s // tb))
    weights = [
        g_kv.reshape(1, d), wkva.astype(_BF16), g_kv_lat.reshape(1, KV_LORA), w_uk.astype(_BF16),
        w_uv.T.astype(_BF16), g_pre.reshape(1, d), w_in.astype(_BF16), g_q_lat.reshape(1, Q_LORA),
        wuq.T.astype(_BF16),
    ]

    def token_major(width):
        return jax.ShapeDtypeStruct((t, width), _BF16), pl.BlockSpec((tb, width), tok)

    def feature_major(width):
        return (jax.ShapeDtypeStruct((t // tb, width, tb), _BF16),
                pl.BlockSpec((1, width, tb), lambda i: (i, 0, 0)))

    outs = [feature_major(MLA_HEADS * MLA_QK_PAD), token_major(MLA_HEADS * MLA_NOPE),
            token_major(LANES), feature_major(MLA_WIDTH), token_major(MLA_WIDTH)]
    qt, kn, kr, vt, sg = pl.pallas_call(
        _projection_kernel,
        out_shape=[o[0] for o in outs],
        grid=(t // tb,),
        in_specs=[pl.BlockSpec((tb, d), tok)]
        + [_single_buffered(w.shape, const) for w in weights]
        + [pl.BlockSpec((tb, LANES), pos), pl.BlockSpec((tb, LANES), pos),
           pl.BlockSpec((LANES, tb), pos_t), pl.BlockSpec((LANES, tb), pos_t)],
        out_specs=[o[1] for o in outs],
        compiler_params=pltpu.CompilerParams(
            dimension_semantics=("parallel",),
            vmem_limit_bytes=VMEM_LIMIT_BYTES),
        name="latent_projections",
    )(h.reshape(t, d), *weights, cos_t, sin_t, cos_t.T, sin_t.T)
    kn, kr, sg = [o.reshape(b, s, -1) for o in (kn, kr, sg)]
    return (qt.reshape(b, s // tb, MLA_HEADS * MLA_QK_PAD, tb), kn, kr,
            vt.reshape(b, s // tb, MLA_WIDTH, tb), sg)


def _attention_kernel(qt_ref, kn_ref, kr_ref, vt_ref, sg_ref, h_ref, wout_ref, gpost_ref, o_ref,
                      m_s, acc_s, y_s):
    blk = ATTN_BLOCK
    qi = pl.program_id(1)
    key_chunk = lax.broadcasted_iota(jnp.int32, (blk, blk), 0) // CHUNK
    query_chunk = lax.broadcasted_iota(jnp.int32, (blk, blk), 1) // CHUNK
    visible = key_chunk <= query_chunk
    ones = jnp.ones((ONES_ROWS, blk), _BF16)

    for h0 in range(0, MLA_HEADS, HEAD_GROUP):
        heads = range(h0, h0 + HEAD_GROUP)
        m_s[...] = jnp.full_like(m_s, -jnp.inf)
        acc_s[...] = jnp.zeros_like(acc_s)

        def step(ki, diagonal):
            rows = pl.ds(pl.multiple_of(ki * blk, blk), blk)
            k_rope = kr_ref[0, rows, :]
            def score(h):
                qt = qt_ref[0, 0, h * MLA_QK_PAD:(h + 1) * MLA_QK_PAD, :]
                k = jnp.concatenate([kn_ref[0, rows, h * MLA_NOPE:(h + 1) * MLA_NOPE], k_rope], axis=-1)
                return _dot(k, qt)

            scores = [score(h) for h in heads[:SCORE_LOOKAHEAD]]
            for i, h in enumerate(heads):
                if i + SCORE_LOOKAHEAD < len(heads):
                    scores.append(score(heads[i + SCORE_LOOKAHEAD]))
                st = scores[i]
                if diagonal:
                    st = jnp.where(visible, st, -jnp.inf)
                m_prev = m_s[i]
                m_new = jnp.maximum(m_prev, jnp.max(st, axis=0, keepdims=True))
                pt = jnp.exp2(st - m_new).astype(_BF16)
                alpha = jnp.exp2(m_prev - m_new)
                vt = jnp.concatenate([vt_ref[0, ki, h * MLA_V:(h + 1) * MLA_V, :], ones], axis=0)
                acc_s[i] = alpha * acc_s[i] + _dot(vt, pt)
                m_s[i] = m_new

        def body(ki, carry):
            step(ki, diagonal=False)
            return carry

        lax.fori_loop(0, qi, body, 0)
        step(qi, diagonal=True)
        for i, h in enumerate(heads):
            head = slice(h * MLA_V, (h + 1) * MLA_V)
            o = (acc_s[i, 0:MLA_V, :] / acc_s[i, MLA_V:MLA_V + 1, :]).T
            y_s[:, head] = (sg_ref[0, :, head].astype(_F32) * o).astype(_BF16)

    y = _dot(y_s[...], wout_ref[...])
    o_ref[0] = h_ref[0] + y * _rms_scale(y) * gpost_ref[...]


def _attention_layer(h, qt, kn, kr, vt, sg, w_out, g_post):
    b, s, d = h.shape
    blk = ATTN_BLOCK
    assert vt.shape == (b, s // blk, MLA_WIDTH, blk)
    assert qt.shape == (b, s // blk, MLA_HEADS * MLA_QK_PAD, blk)
    whole = lambda i, j: (i, 0, 0)
    qblk = lambda i, j: (i, j, 0)
    const = lambda i, j: (0, 0)
    return pl.pallas_call(
        _attention_kernel,
        out_shape=jax.ShapeDtypeStruct((b, s, d), _F32),
        grid=(b, s // blk),
        in_specs=[
            pl.BlockSpec((1, 1) + qt.shape[2:], lambda i, j: (i, j, 0, 0)),
            _single_buffered((1, s, kn.shape[-1]), whole),
            _single_buffered((1, s, kr.shape[-1]), whole),
            _single_buffered((1,) + vt.shape[1:], lambda i, j: (i, 0, 0, 0)),
            pl.BlockSpec((1, blk, MLA_WIDTH), qblk),
            pl.BlockSpec((1, blk, d), qblk),
            _single_buffered(w_out.shape, const),
            _single_buffered((1, d), const),
        ],
        out_specs=pl.BlockSpec((1, blk, d), qblk),
        scratch_shapes=[
            pltpu.VMEM((HEAD_GROUP, 1, blk), _F32),
            pltpu.VMEM((HEAD_GROUP, MLA_V + ONES_ROWS, blk), _F32),
            pltpu.VMEM((blk, MLA_WIDTH), _BF16),
        ],
        compiler_params=pltpu.CompilerParams(
            dimension_semantics=("parallel", "arbitrary"),
            vmem_limit_bytes=VMEM_LIMIT_BYTES),
        name="latent_attention",
    )(qt, kn, kr, vt, sg, h, w_out.astype(_BF16), g_post.reshape(1, d))


@jax.jit
def kernel(x, g_pre_a, w_in_a, gn_gain_a, w_out_a, g_post_a, g_kv, w_kv_a, g_kv_lat, w_uk, w_uv,
           g_pre_b, w_in_b, g_q_lat, w_uq, w_out_b, g_post_b):
    assert w_in_a.shape[0] == 1 and w_in_b.shape[0] == 1
    h = _retention_layer(x, g_pre_a[0], w_in_a[0], gn_gain_a[0], w_out_a[0], g_post_a[0])
    qt, kn, kr, vt, sg = _projections(h, g_kv, w_kv_a, g_kv_lat, w_uk, w_uv,
                                      g_pre_b[0], w_in_b[0], g_q_lat[0], w_uq[0])
    return _attention_layer(h, qt, kn, kr, vt, sg, w_out_b[0], g_post_b[0])
```

```python
import functools

import jax
import jax.numpy as jnp
from jax import lax
from jax.experimental import pallas as pl
from jax.experimental.pallas import tpu as pltpu

D_MODEL = 1024
CHUNK = 64
RMS_EPS = 1e-6
ROPE_BASE = 10000.0

RET_HEADS = 8
RET_QK = 128
RET_V = 256
RET_QK_WIDTH = RET_HEADS * RET_QK
RET_WIDTH = RET_HEADS * RET_V

MLA_HEADS = 8
MLA_NOPE = 128
MLA_ROPE = 64
MLA_V = 128
MLA_WIDTH = MLA_HEADS * MLA_V
Q_LORA = 384
KV_LORA = 256

LANES = 128
MLA_QK_PAD = 2 * LANES
VMEM_LIMIT_BYTES = 56 * 1024 * 1024

RET_TOKEN_BLOCK = 512
RET_SPAN = 256
PROJ_TOKEN_BLOCK = 512
ATTN_BLOCK = PROJ_TOKEN_BLOCK
SCORE_LOOKAHEAD = 2
ONES_ROWS = 16
EXP2_RANGE = 16.0

LOG2_E = 1.4426950408889634

_BF16 = jnp.bfloat16
_F32 = jnp.float32


def _single_buffered(shape, index_map):
    return pl.BlockSpec(shape, index_map, pipeline_mode=pl.Buffered(1))


def _rms_scale(x):
    return lax.rsqrt(jnp.mean(x * x, axis=-1, keepdims=True) + RMS_EPS)


def _rotate(x, cos, sin_signed):
    return x * cos + pltpu.roll(x, LANES // 2, axis=1) * sin_signed


def _dot(a, b):
    return jnp.dot(a, b, preferred_element_type=_F32)


def _dot_nt(a, b):
    return lax.dot_general(a, b, (((1,), (1,)), ((), ())), preferred_element_type=_F32)


def _dot_tn(a, b):
    return lax.dot_general(a, b, (((0,), (0,)), ((), ())), preferred_element_type=_F32)


def _retention_kernel(x_ref, gpre_ref, win_ref, cos_ref, sin_ref, dmask_ref, xi_ref, zeta_ref,
                      gspan_ref, gn_ref, wout_ref, gpost_ref, o_ref,
                      state_ref, q_s, k_s, v_s, gate_s, y_s):
    @pl.when(pl.program_id(1) == 0)
    def _():
        state_ref[...] = jnp.zeros_like(state_ref)

    x = x_ref[0]
    hb = (x * _rms_scale(x) * gpre_ref[...]).astype(_BF16)
    cos = cos_ref[...]
    sin = sin_ref[...]

    q_all = _dot(hb, win_ref[:, 0:RET_QK_WIDTH])
    for h in range(RET_HEADS):
        q_s[h] = _rotate(q_all[:, h * RET_QK:(h + 1) * RET_QK], cos, sin)
    k_all = _dot(hb, win_ref[:, RET_QK_WIDTH:2 * RET_QK_WIDTH])
    for h in range(RET_HEADS):
        k_s[h] = _rotate(k_all[:, h * RET_QK:(h + 1) * RET_QK], cos, sin)
    v_s[...] = _dot(hb, win_ref[:, 2 * RET_QK_WIDTH:2 * RET_QK_WIDTH + RET_WIDTH]).astype(_BF16)
    gate_s[...] = _dot(hb, win_ref[:, 2 * RET_QK_WIDTH + RET_WIDTH:])

    for s in range(RET_TOKEN_BLOCK // RET_SPAN):
        rows = slice(s * RET_SPAN, (s + 1) * RET_SPAN)
        for h in range(RET_HEADS):
            cols = slice(h * RET_V, (h + 1) * RET_V)
            q = q_s[h, rows, :]
            k = k_s[h, rows, :]
            v = v_s[rows, cols]
            scores = _dot_nt(q.astype(_BF16), k.astype(_BF16)) * dmask_ref[h]
            state = state_ref[h]
            o = _dot(scores.astype(_BF16), v)
            o = o + _dot((q * xi_ref[h]).astype(_BF16), state.astype(_BF16))
            state_ref[h] = state * gspan_ref[h] + _dot_tn((k * zeta_ref[h]).astype(_BF16), v)
            mu = jnp.mean(o, axis=-1, keepdims=True)
            oc = o - mu
            var = jnp.mean(oc * oc, axis=-1, keepdims=True)
            on = oc * lax.rsqrt(var + RMS_EPS) * gn_ref[:, cols]
            gate = gate_s[rows, cols]
            silu = gate / (1.0 + jnp.exp(-gate))
            y_s[rows, cols] = (silu * on).astype(_BF16)

    y = _dot(y_s[...], wout_ref[...])
    o_ref[0] = x + y * _rms_scale(y) * gpost_ref[...]


def _retention_tables():
    log_g = jnp.log(1.0 - jnp.exp2(-5.0 - jnp.arange(RET_HEADS, dtype=_F32)))
    idx = jnp.arange(RET_SPAN, dtype=_F32)
    diff = idx[:, None] - idx[None, :]
    chunk = jnp.arange(RET_SPAN) // CHUNK
    same = chunk[:, None] == chunk[None, :]
    earlier = chunk[None, :] < chunk[:, None]
    dist = jnp.where(same, jnp.abs(diff), diff)
    decay = jnp.exp(log_g[:, None, None] * dist[None])
    k_scale = RET_QK ** -0.5
    dmask = jnp.where((same | earlier)[None], decay, 0.0) * k_scale
    xi = jnp.exp(log_g[:, None] * (idx + 1.0)[None]) * k_scale
    zeta = jnp.exp(log_g[:, None] * (RET_SPAN - 1.0 - idx)[None])
    g_span = jnp.exp(log_g * RET_SPAN)
    xi = jnp.broadcast_to(xi[:, :, None], (RET_HEADS, RET_SPAN, RET_QK))
    zeta = jnp.broadcast_to(zeta[:, :, None], (RET_HEADS, RET_SPAN, RET_QK))
    g_span = jnp.broadcast_to(g_span[:, None, None], (RET_HEADS, 1, RET_V))
    return dmask.astype(_F32), xi.astype(_F32), zeta.astype(_F32), g_span.astype(_F32)


def _rope_angles(seq, half):
    inv = ROPE_BASE ** (-jnp.arange(half, dtype=_F32) / half)
    ang = jnp.arange(seq, dtype=_F32)[:, None] * inv[None, :]
    return jnp.cos(ang), jnp.sin(ang)


def _retention_layer(x, g_pre, w_in, gn_gain, w_out, g_post):
    b, s, d = x.shape
    tb = RET_TOKEN_BLOCK
    cos, sin = _rope_angles(s, RET_QK // 2)
    cos_t = jnp.concatenate([cos, cos], axis=-1)
    sin_t = jnp.concatenate([-sin, sin], axis=-1)
    dmask, xi, zeta, g_span = _retention_tables()
    const2 = lambda i, j: (0, 0)
    const3 = lambda i, j: (0, 0, 0)
    return pl.pallas_call(
        _retention_kernel,
        out_shape=jax.ShapeDtypeStruct((b, s, d), _F32),
        grid=(b, s // tb),
        in_specs=[
            pl.BlockSpec((1, tb, d), lambda i, j: (i, j, 0)),
            _single_buffered((1, d), const2),
            _single_buffered(w_in.shape, const2),
            pl.BlockSpec((tb, RET_QK), lambda i, j: (j, 0)),
            pl.BlockSpec((tb, RET_QK), lambda i, j: (j, 0)),
            _single_buffered(dmask.shape, const3),
            _single_buffered(xi.shape, const3),
            _single_buffered(zeta.shape, const3),
            _single_buffered(g_span.shape, const3),
            _single_buffered((1, RET_WIDTH), const2),
            _single_buffered(w_out.shape, const2),
            _single_buffered((1, d), const2),
        ],
        out_specs=pl.BlockSpec((1, tb, d), lambda i, j: (i, j, 0)),
        scratch_shapes=[
            pltpu.VMEM((RET_HEADS, RET_QK, RET_V), _F32),
            pltpu.VMEM((RET_HEADS, tb, RET_QK), _F32),
            pltpu.VMEM((RET_HEADS, tb, RET_QK), _F32),
            pltpu.VMEM((tb, RET_WIDTH), _BF16),
            pltpu.VMEM((tb, RET_WIDTH), _F32),
            pltpu.VMEM((tb, RET_WIDTH), _BF16),
        ],
        compiler_params=pltpu.CompilerParams(
            dimension_semantics=("parallel", "arbitrary"),
            vmem_limit_bytes=VMEM_LIMIT_BYTES),
        name="retention_layer",
    )(x, g_pre.reshape(1, d), w_in.astype(_BF16), cos_t, sin_t, dmask, xi, zeta, g_span,
      gn_gain.reshape(1, RET_WIDTH), w_out.astype(_BF16), g_post.reshape(1, d))


def _projection_kernel(h_ref, gkv_ref, wkva_ref, gkvlat_ref, wuk_ref, wuvt_ref, gpre_ref, winb_ref,
                       gqlat_ref, wuqt_ref, cos_ref, sin_ref, cost_ref, sint_ref,
                       qt_ref, kn_ref, kr_ref, vt_ref, sg_ref):
    x = h_ref[...]
    xn = x * _rms_scale(x)
    cos = cos_ref[...]
    sin = sin_ref[...]

    a = _dot((xn * gkv_ref[...]).astype(_BF16), wkva_ref[...])
    c_kv = a[:, :KV_LORA]
    kr_ref[...] = _rotate(a[:, KV_LORA:], cos, sin).astype(_BF16)
    c_kv = (c_kv * _rms_scale(c_kv) * gkvlat_ref[...]).astype(_BF16)
    kn_ref[...] = _dot(c_kv, wuk_ref[...]).astype(_BF16)
    vt_ref[0] = _dot_nt(wuvt_ref[...], c_kv).astype(_BF16)

    proj = _dot((xn * gpre_ref[...]).astype(_BF16), winb_ref[...])
    gate = proj[:, Q_LORA:]
    sg_ref[...] = (gate / (1.0 + jnp.exp(-gate))).astype(_BF16)
    c_q = proj[:, :Q_LORA]
    c_q = (c_q * _rms_scale(c_q) * gqlat_ref[...]).astype(_BF16)
    qt = _dot_nt(wuqt_ref[...], c_q)
    scale = (MLA_NOPE + MLA_ROPE) ** -0.5 * LOG2_E
    cos_rows = cost_ref[...]
    sin_rows = sint_ref[...]
    for h in range(MLA_HEADS):
        lo = h * MLA_QK_PAD
        qt_ref[0, lo:lo + LANES, :] = (qt[lo:lo + LANES] * scale).astype(_BF16)
        r = qt[lo + LANES:lo + 2 * LANES]
        partner = jnp.concatenate([r[LANES // 2:], r[:LANES // 2]], axis=0)
        qt_ref[0, lo + LANES:lo + 2 * LANES, :] = (
            (r * cos_rows + partner * sin_rows) * scale).astype(_BF16)


def _spread_rope_columns(w):
    half = MLA_ROPE // 2
    z = jnp.zeros(w.shape[:-1] + (LANES // 2 - half,), w.dtype)
    return jnp.concatenate([w[..., :half], z, w[..., half:], z], axis=-1)


def _projections(h, g_kv, w_kv_a, g_kv_lat, w_uk, w_uv, g_pre, w_in, g_q_lat, w_uq):
    b, s, d = h.shape
    t = b * s
    tb = PROJ_TOKEN_BLOCK
    half = MLA_ROPE // 2
    cos, sin = _rope_angles(s, half)
    ones = jnp.ones((s, LANES // 2 - half), _F32)
    zeros = jnp.zeros((s, LANES // 2 - half), _F32)
    cos_t = jnp.concatenate([cos, ones, cos, ones], axis=-1)
    sin_t = jnp.concatenate([-sin, zeros, sin, zeros], axis=-1)

    wkva = jnp.concatenate([w_kv_a[:, :KV_LORA], _spread_rope_columns(w_kv_a[:, KV_LORA:])], axis=-1)
    wuq = w_uq.reshape(Q_LORA, MLA_HEADS, MLA_NOPE + MLA_ROPE)
    wuq = jnp.concatenate([wuq[..., :MLA_NOPE], _spread_rope_columns(wuq[..., MLA_NOPE:])], axis=-1)
    wuq = wuq.reshape(Q_LORA, MLA_HEADS * MLA_QK_PAD)

    const = lambda i: (0, 0)
    tok = lambda i: (i, 0)
    pos = lambda i: (i % (s // tb), 0)
    pos_t = lambda i: (0, i % (s // tb))
    weights = [
        g_kv.reshape(1, d), wkva.astype(_BF16), g_kv_lat.reshape(1, KV_LORA), w_uk.astype(_BF16),
        w_uv.T.astype(_BF16), g_pre.reshape(1, d), w_in.astype(_BF16), g_q_lat.reshape(1, Q_LORA),
        wuq.T.astype(_BF16),
    ]

    def token_major(width):
        return jax.ShapeDtypeStruct((t, width), _BF16), pl.BlockSpec((tb, width), tok)

    def feature_major(width):
        return (jax.ShapeDtypeStruct((t // tb, width, tb), _BF16),
                pl.BlockSpec((1, width, tb), lambda i: (i, 0, 0)))

    outs = [feature_major(MLA_HEADS * MLA_QK_PAD), token_major(MLA_HEADS * MLA_NOPE),
            token_major(LANES), feature_major(MLA_WIDTH), token_major(MLA_WIDTH)]
    qt, kn, kr, vt, sg = pl.pallas_call(
        _projection_kernel,
        out_shape=[o[0] for o in outs],
        grid=(t // tb,),
        in_specs=[pl.BlockSpec((tb, d), tok)]
        + [_single_buffered(w.shape, const) for w in weights]
        + [pl.BlockSpec((tb, LANES), pos), pl.BlockSpec((tb, LANES), pos),
           pl.BlockSpec((LANES, tb), pos_t), pl.BlockSpec((LANES, tb), pos_t)],
        out_specs=[o[1] for o in outs],
        compiler_params=pltpu.CompilerParams(
            dimension_semantics=("parallel",),
            vmem_limit_bytes=VMEM_LIMIT_BYTES),
        name="latent_projections",
    )(h.reshape(t, d), *weights, cos_t, sin_t, cos_t.T, sin_t.T)
    kn, kr, sg = [o.reshape(b, s, -1) for o in (kn, kr, sg)]
    return (qt.reshape(b, s // tb, MLA_HEADS * MLA_QK_PAD, tb), kn, kr,
            vt.reshape(b, s // tb, MLA_WIDTH, tb), sg)


def _attention_kernel(qt_ref, kn_ref, kr_ref, vt_ref, sg_ref, h_ref, wout_ref, gpost_ref, o_ref,
                      m_s, g_s, acc_s, y_s):
    blk = ATTN_BLOCK
    qi = pl.program_id(1)
    key_chunk = lax.broadcasted_iota(jnp.int32, (blk, blk), 0) // CHUNK
    query_chunk = lax.broadcasted_iota(jnp.int32, (blk, blk), 1) // CHUNK
    visible = key_chunk <= query_chunk
    ones = jnp.ones((ONES_ROWS, blk), _BF16)

    heads = range(MLA_HEADS)

    def head_scores(ki):
        rows = pl.ds(pl.multiple_of(ki * blk, blk), blk)
        k_rope = kr_ref[0, rows, :]

        def score(h):
            qt = qt_ref[0, 0, h * MLA_QK_PAD:(h + 1) * MLA_QK_PAD, :]
            k = jnp.concatenate([kn_ref[0, rows, h * MLA_NOPE:(h + 1) * MLA_NOPE], k_rope], axis=-1)
            return _dot(k, qt)

        scores = [score(h) for h in heads[:SCORE_LOOKAHEAD]]
        for h in heads:
            if h + SCORE_LOOKAHEAD < MLA_HEADS:
                scores.append(score(h + SCORE_LOOKAHEAD))
            yield h, scores[h]

    def values(ki, h):
        return jnp.concatenate([vt_ref[0, ki, h * MLA_V:(h + 1) * MLA_V, :], ones], axis=0)

    def running_max_step(ki, carry=None):
        seen = jnp.logical_or(visible, ki < qi)
        for h, st in head_scores(ki):
            st = jnp.where(seen, st, -jnp.inf)
            m_prev = m_s[h]
            m_new = jnp.maximum(m_prev, jnp.max(st, axis=0, keepdims=True))
            pt = jnp.exp2(st - m_new).astype(_BF16)
            alpha = jnp.exp2(m_prev - m_new)
            acc_s[h] = alpha * acc_s[h] + _dot(values(ki, h), pt)
            m_s[h] = m_new
        return carry

    def fixed_reference_first_step():
        seen = jnp.logical_or(visible, qi > 0)
        for h, st in head_scores(0):
            st = jnp.where(seen, st, -jnp.inf)
            m_ref = jnp.max(st[0:CHUNK], axis=0, keepdims=True)
            m_s[h] = m_ref
            g_s[h] = jnp.max(st, axis=0, keepdims=True)
            pt = jnp.exp2(st - m_ref).astype(_BF16)
            acc_s[h] = _dot(values(0, h), pt)

    def fixed_reference_step(ki, diagonal):
        for h, st in head_scores(ki):
            if diagonal:
                st = jnp.where(visible, st, -jnp.inf)
            g_s[h] = jnp.maximum(g_s[h], jnp.max(st, axis=0, keepdims=True))
            pt = jnp.exp2(st - m_s[h]).astype(_BF16)
            acc_s[h] = acc_s[h] + _dot(values(ki, h), pt)

    def reset():
        m_s[...] = jnp.full_like(m_s, -jnp.inf)
        acc_s[...] = jnp.zeros_like(acc_s)

    def finish():
        for h in heads:
            head = slice(h * MLA_V, (h + 1) * MLA_V)
            o = (acc_s[h, 0:MLA_V, :] / acc_s[h, MLA_V:MLA_V + 1, :]).T
            y_s[:, head] = (sg_ref[0, :, head].astype(_F32) * o).astype(_BF16)

    fixed_reference_first_step()
    lax.fori_loop(1, qi, lambda ki, c: (fixed_reference_step(ki, diagonal=False), c)[1], 0)

    @pl.when(qi > 0)
    def _():
        fixed_reference_step(qi, diagonal=True)

    finish()

    in_range = jnp.max(g_s[...] - m_s[...]) <= EXP2_RANGE

    @pl.when(jnp.logical_not(in_range))
    def _():
        reset()
        lax.fori_loop(0, qi + 1, running_max_step, 0)
        finish()

    y = _dot(y_s[...], wout_ref[...])
    o_ref[0] = h_ref[0] + y * _rms_scale(y) * gpost_ref[...]


def _attention_layer(h, qt, kn, kr, vt, sg, w_out, g_post):
    b, s, d = h.shape
    blk = ATTN_BLOCK
    assert vt.shape == (b, s // blk, MLA_WIDTH, blk)
    assert qt.shape == (b, s // blk, MLA_HEADS * MLA_QK_PAD, blk)
    whole = lambda i, j: (i, 0, 0)
    qblk = lambda i, j: (i, j, 0)
    const = lambda i, j: (0, 0)
    return pl.pallas_call(
        _attention_kernel,
        out_shape=jax.ShapeDtypeStruct((b, s, d), _F32),
        grid=(b, s // blk),
        in_specs=[
            pl.BlockSpec((1, 1) + qt.shape[2:], lambda i, j: (i, j, 0, 0)),
            _single_buffered((1, s, kn.shape[-1]), whole),
            _single_buffered((1, s, kr.shape[-1]), whole),
            _single_buffered((1,) + vt.shape[1:], lambda i, j: (i, 0, 0, 0)),
            pl.BlockSpec((1, blk, MLA_WIDTH), qblk),
            pl.BlockSpec((1, blk, d), qblk),
            _single_buffered(w_out.shape, const),
            _single_buffered((1, d), const),
        ],
        out_specs=pl.BlockSpec((1, blk, d), qblk),
        scratch_shapes=[
            pltpu.VMEM((MLA_HEADS, 1, blk), _F32),
            pltpu.VMEM((MLA_HEADS, 1, blk), _F32),
            pltpu.VMEM((MLA_HEADS, MLA_V + ONES_ROWS, blk), _F32),
            pltpu.VMEM((blk, MLA_WIDTH), _BF16),
        ],
        compiler_params=pltpu.CompilerParams(
            dimension_semantics=("parallel", "arbitrary"),
            vmem_limit_bytes=VMEM_LIMIT_BYTES),
        name="latent_attention",
    )(qt, kn, kr, vt, sg, h, w_out.astype(_BF16), g_post.reshape(1, d))


@jax.jit
def kernel(x, g_pre_a, w_in_a, gn_gain_a, w_out_a, g_post_a, g_kv, w_kv_a, g_kv_lat, w_uk, w_uv,
           g_pre_b, w_in_b, g_q_lat, w_uq, w_out_b, g_post_b):
    assert w_in_a.shape[0] == 1 and w_in_b.shape[0] == 1
    h = _retention_layer(x, g_pre_a[0], w_in_a[0], gn_gain_a[0], w_out_a[0], g_post_a[0])
    qt, kn, kr, vt, sg = _projections(h, g_kv, w_kv_a, g_kv_lat, w_uk, w_uv,
                                      g_pre_b[0], w_in_b[0], g_q_lat[0], w_uq[0])
    return _attention_layer(h, qt, kn, kr, vt, sg, w_out_b[0], g_post_b[0])
```

```python
import functools

import jax
import jax.numpy as jnp
from jax import lax
from jax.experimental import pallas as pl
from jax.experimental.pallas import tpu as pltpu

D_MODEL = 1024
CHUNK = 64
RMS_EPS = 1e-6
ROPE_BASE = 10000.0

RET_HEADS = 8
RET_QK = 128
RET_V = 256
RET_QK_WIDTH = RET_HEADS * RET_QK
RET_WIDTH = RET_HEADS * RET_V

MLA_HEADS = 8
MLA_NOPE = 128
MLA_ROPE = 64
MLA_V = 128
MLA_WIDTH = MLA_HEADS * MLA_V
Q_LORA = 384
KV_LORA = 256

LANES = 128
MLA_QK_PAD = 2 * LANES
VMEM_BYTES_V7X = 64 * 1024 * 1024
VMEM_LIMIT_BYTES = VMEM_BYTES_V7X - 4 * 1024 * 1024

RET_TOKEN_BLOCK = 512
RET_SPAN = 256
PROJ_TOKEN_BLOCK = 512
ATTN_BLOCK = PROJ_TOKEN_BLOCK
SCORE_LOOKAHEAD = 2
ONES_ROWS = 16
EXP2_RANGE = 16.0

LOG2_E = 1.4426950408889634

_BF16 = jnp.bfloat16
_F32 = jnp.float32


def _single_buffered(shape, index_map):
    return pl.BlockSpec(shape, index_map, pipeline_mode=pl.Buffered(1))


def _rms_scale(x):
    return lax.rsqrt(jnp.mean(x * x, axis=-1, keepdims=True) + RMS_EPS)


def _rotate(x, cos, sin_signed):
    return x * cos + pltpu.roll(x, LANES // 2, axis=1) * sin_signed


def _dot(a, b):
    return jnp.dot(a, b, preferred_element_type=_F32)


def _dot_nt(a, b):
    return lax.dot_general(a, b, (((1,), (1,)), ((), ())), preferred_element_type=_F32)


def _dot_tn(a, b):
    return lax.dot_general(a, b, (((0,), (0,)), ((), ())), preferred_element_type=_F32)


def _retention_kernel(x_ref, gpre_ref, win_ref, cos_ref, sin_ref, dmask_ref, xi_ref, zeta_ref,
                      gspan_ref, gn_ref, wout_ref, gpost_ref, o_ref,
                      state_ref, q_s, k_s, v_s, gate_s, y_s):
    @pl.when(pl.program_id(1) == 0)
    def _():
        state_ref[...] = jnp.zeros_like(state_ref)

    x = x_ref[0]
    hb = (x * _rms_scale(x) * gpre_ref[...]).astype(_BF16)
    cos = cos_ref[...]
    sin = sin_ref[...]

    q_all = _dot(hb, win_ref[:, 0:RET_QK_WIDTH])
    for h in range(RET_HEADS):
        q_s[h] = _rotate(q_all[:, h * RET_QK:(h + 1) * RET_QK], cos, sin)
    k_all = _dot(hb, win_ref[:, RET_QK_WIDTH:2 * RET_QK_WIDTH])
    for h in range(RET_HEADS):
        k_s[h] = _rotate(k_all[:, h * RET_QK:(h + 1) * RET_QK], cos, sin)
    v_s[...] = _dot(hb, win_ref[:, 2 * RET_QK_WIDTH:2 * RET_QK_WIDTH + RET_WIDTH]).astype(_BF16)
    gate_s[...] = _dot(hb, win_ref[:, 2 * RET_QK_WIDTH + RET_WIDTH:])

    for s in range(RET_TOKEN_BLOCK // RET_SPAN):
        rows = slice(s * RET_SPAN, (s + 1) * RET_SPAN)
        for h in range(RET_HEADS):
            cols = slice(h * RET_V, (h + 1) * RET_V)
            q = q_s[h, rows, :]
            k = k_s[h, rows, :]
            v = v_s[rows, cols]
            scores = _dot_nt(q.astype(_BF16), k.astype(_BF16)) * dmask_ref[h]
            state = state_ref[h]
            o = _dot(scores.astype(_BF16), v)
            o = o + _dot((q * xi_ref[h]).astype(_BF16), state.astype(_BF16))
            state_ref[h] = state * gspan_ref[h] + _dot_tn((k * zeta_ref[h]).astype(_BF16), v)
            mu = jnp.mean(o, axis=-1, keepdims=True)
            oc = o - mu
            var = jnp.mean(oc * oc, axis=-1, keepdims=True)
            on = oc * lax.rsqrt(var + RMS_EPS) * gn_ref[:, cols]
            gate = gate_s[rows, cols]
            silu = gate / (1.0 + jnp.exp(-gate))
            y_s[rows, cols] = (silu * on).astype(_BF16)

    y = _dot(y_s[...], wout_ref[...])
    o_ref[0] = x + y * _rms_scale(y) * gpost_ref[...]


def _retention_tables():
    log_g = jnp.log(1.0 - jnp.exp2(-5.0 - jnp.arange(RET_HEADS, dtype=_F32)))
    idx = jnp.arange(RET_SPAN, dtype=_F32)
    diff = idx[:, None] - idx[None, :]
    chunk = jnp.arange(RET_SPAN) // CHUNK
    same = chunk[:, None] == chunk[None, :]
    earlier = chunk[None, :] < chunk[:, None]
    dist = jnp.where(same, jnp.abs(diff), diff)
    decay = jnp.exp(log_g[:, None, None] * dist[None])
    k_scale = RET_QK ** -0.5
    dmask = jnp.where((same | earlier)[None], decay, 0.0) * k_scale
    xi = jnp.exp(log_g[:, None] * (idx + 1.0)[None]) * k_scale
    zeta = jnp.exp(log_g[:, None] * (RET_SPAN - 1.0 - idx)[None])
    g_span = jnp.exp(log_g * RET_SPAN)
    xi = jnp.broadcast_to(xi[:, :, None], (RET_HEADS, RET_SPAN, RET_QK))
    zeta = jnp.broadcast_to(zeta[:, :, None], (RET_HEADS, RET_SPAN, RET_QK))
    g_span = jnp.broadcast_to(g_span[:, None, None], (RET_HEADS, 1, RET_V))
    return dmask.astype(_F32), xi.astype(_F32), zeta.astype(_F32), g_span.astype(_F32)


def _rope_angles(seq, half):
    inv = ROPE_BASE ** (-jnp.arange(half, dtype=_F32) / half)
    ang = jnp.arange(seq, dtype=_F32)[:, None] * inv[None, :]
    return jnp.cos(ang), jnp.sin(ang)


def _retention_layer(x, g_pre, w_in, gn_gain, w_out, g_post):
    b, s, d = x.shape
    tb = RET_TOKEN_BLOCK
    cos, sin = _rope_angles(s, RET_QK // 2)
    cos_t = jnp.concatenate([cos, cos], axis=-1)
    sin_t = jnp.concatenate([-sin, sin], axis=-1)
    dmask, xi, zeta, g_span = _retention_tables()
    const2 = lambda i, j: (0, 0)
    const3 = lambda i, j: (0, 0, 0)
    return pl.pallas_call(
        _retention_kernel,
        out_shape=jax.ShapeDtypeStruct((b, s, d), _F32),
        grid=(b, s // tb),
        in_specs=[
            pl.BlockSpec((1, tb, d), lambda i, j: (i, j, 0)),
            _single_buffered((1, d), const2),
            _single_buffered(w_in.shape, const2),
            pl.BlockSpec((tb, RET_QK), lambda i, j: (j, 0)),
            pl.BlockSpec((tb, RET_QK), lambda i, j: (j, 0)),
            _single_buffered(dmask.shape, const3),
            _single_buffered(xi.shape, const3),
            _single_buffered(zeta.shape, const3),
            _single_buffered(g_span.shape, const3),
            _single_buffered((1, RET_WIDTH), const2),
            _single_buffered(w_out.shape, const2),
            _single_buffered((1, d), const2),
        ],
        out_specs=pl.BlockSpec((1, tb, d), lambda i, j: (i, j, 0)),
        scratch_shapes=[
            pltpu.VMEM((RET_HEADS, RET_QK, RET_V), _F32),
            pltpu.VMEM((RET_HEADS, tb, RET_QK), _F32),
            pltpu.VMEM((RET_HEADS, tb, RET_QK), _F32),
            pltpu.VMEM((tb, RET_WIDTH), _BF16),
            pltpu.VMEM((tb, RET_WIDTH), _F32),
            pltpu.VMEM((tb, RET_WIDTH), _BF16),
        ],
        compiler_params=pltpu.CompilerParams(
            dimension_semantics=("parallel", "arbitrary"),
            vmem_limit_bytes=VMEM_LIMIT_BYTES),
        name="retention_layer",
    )(x, g_pre.reshape(1, d), w_in.astype(_BF16), cos_t, sin_t, dmask, xi, zeta, g_span,
      gn_gain.reshape(1, RET_WIDTH), w_out.astype(_BF16), g_post.reshape(1, d))


def _projection_kernel(h_ref, gkv_ref, wkva_ref, gkvlat_ref, wuk_ref, wuvt_ref, gpre_ref, winb_ref,
                       gqlat_ref, wuqt_ref, cos_ref, sin_ref, cost_ref, sint_ref,
                       qt_ref, kn_ref, kr_ref, vt_ref, sg_ref):
    x = h_ref[...]
    xn = x * _rms_scale(x)
    cos = cos_ref[...]
    sin = sin_ref[...]

    a = _dot((xn * gkv_ref[...]).astype(_BF16), wkva_ref[...])
    c_kv = a[:, :KV_LORA]
    kr_ref[...] = _rotate(a[:, KV_LORA:], cos, sin).astype(_BF16)
    c_kv = (c_kv * _rms_scale(c_kv) * gkvlat_ref[...]).astype(_BF16)
    kn_ref[...] = _dot(c_kv, wuk_ref[...]).astype(_BF16)
    vt_ref[0] = _dot_nt(wuvt_ref[...], c_kv).astype(_BF16)

    proj = _dot((xn * gpre_ref[...]).astype(_BF16), winb_ref[...])
    gate = proj[:, Q_LORA:]
    sg_ref[...] = (gate / (1.0 + jnp.exp(-gate))).astype(_BF16)
    c_q = proj[:, :Q_LORA]
    c_q = (c_q * _rms_scale(c_q) * gqlat_ref[...]).astype(_BF16)
    qt = _dot_nt(wuqt_ref[...], c_q)
    scale = (MLA_NOPE + MLA_ROPE) ** -0.5 * LOG2_E
    cos_rows = cost_ref[...]
    sin_rows = sint_ref[...]
    for h in range(MLA_HEADS):
        lo = h * MLA_QK_PAD
        qt_ref[0, lo:lo + LANES, :] = (qt[lo:lo + LANES] * scale).astype(_BF16)
        r = qt[lo + LANES:lo + 2 * LANES]
        partner = jnp.concatenate([r[LANES // 2:], r[:LANES // 2]], axis=0)
        qt_ref[0, lo + LANES:lo + 2 * LANES, :] = (
            (r * cos_rows + partner * sin_rows) * scale).astype(_BF16)


def _spread_rope_columns(w):
    half = MLA_ROPE // 2
    z = jnp.zeros(w.shape[:-1] + (LANES // 2 - half,), w.dtype)
    return jnp.concatenate([w[..., :half], z, w[..., half:], z], axis=-1)


def _projections(h, g_kv, w_kv_a, g_kv_lat, w_uk, w_uv, g_pre, w_in, g_q_lat, w_uq):
    b, s, d = h.shape
    t = b * s
    tb = PROJ_TOKEN_BLOCK
    half = MLA_ROPE // 2
    cos, sin = _rope_angles(s, half)
    ones = jnp.ones((s, LANES // 2 - half), _F32)
    zeros = jnp.zeros((s, LANES // 2 - half), _F32)
    cos_t = jnp.concatenate([cos, ones, cos, ones], axis=-1)
    sin_t = jnp.concatenate([-sin, zeros, sin, zeros], axis=-1)

    wkva = jnp.concatenate([w_kv_a[:, :KV_LORA], _spread_rope_columns(w_kv_a[:, KV_LORA:])], axis=-1)
    wuq = w_uq.reshape(Q_LORA, MLA_HEADS, MLA_NOPE + MLA_ROPE)
    wuq = jnp.concatenate([wuq[..., :MLA_NOPE], _spread_rope_columns(wuq[..., MLA_NOPE:])], axis=-1)
    wuq = wuq.reshape(Q_LORA, MLA_HEADS * MLA_QK_PAD)

    const = lambda i: (0, 0)
    tok = lambda i: (i, 0)
    pos = lambda i: (i % (s // tb), 0)
    pos_t = lambda i: (0, i % (s // tb))
    weights = [
        g_kv.reshape(1, d), wkva.astype(_BF16), g_kv_lat.reshape(1, KV_LORA), w_uk.astype(_BF16),
        w_uv.T.astype(_BF16), g_pre.reshape(1, d), w_in.astype(_BF16), g_q_lat.reshape(1, Q_LORA),
        wuq.T.astype(_BF16),
    ]

    def token_major(width):
        return jax.ShapeDtypeStruct((t, width), _BF16), pl.BlockSpec((tb, width), tok)

    def feature_major(width):
        return (jax.ShapeDtypeStruct((t // tb, width, tb), _BF16),
                pl.BlockSpec((1, width, tb), lambda i: (i, 0, 0)))

    outs = [feature_major(MLA_HEADS * MLA_QK_PAD), token_major(MLA_HEADS * MLA_NOPE),
            token_major(LANES), feature_major(MLA_WIDTH), token_major(MLA_WIDTH)]
    qt, kn, kr, vt, sg = pl.pallas_call(
        _projection_kernel,
        out_shape=[o[0] for o in outs],
        grid=(t // tb,),
        in_specs=[pl.BlockSpec((tb, d), tok)]
        + [_single_buffered(w.shape, const) for w in weights]
        + [pl.BlockSpec((tb, LANES), pos), pl.BlockSpec((tb, LANES), pos),
           pl.BlockSpec((LANES, tb), pos_t), pl.BlockSpec((LANES, tb), pos_t)],
        out_specs=[o[1] for o in outs],
        compiler_params=pltpu.CompilerParams(
            dimension_semantics=("parallel",),
            vmem_limit_bytes=VMEM_LIMIT_BYTES),
        name="latent_projections",
    )(h.reshape(t, d), *weights, cos_t, sin_t, cos_t.T, sin_t.T)
    kn, kr, sg = [o.reshape(b, s, -1) for o in (kn, kr, sg)]
    return (qt.reshape(b, s // tb, MLA_HEADS * MLA_QK_PAD, tb), kn, kr,
            vt.reshape(b, s // tb, MLA_WIDTH, tb), sg)


def _attention_kernel(qt_ref, kn_ref, kr_ref, vt_ref, sg_ref, h_ref, wout_ref, gpost_ref, o_ref,
                      m_s, g_s, acc_s, y_s):
    blk = ATTN_BLOCK
    qi = pl.program_id(1)
    key_chunk = lax.broadcasted_iota(jnp.int32, (blk, blk), 0) // CHUNK
    query_chunk = lax.broadcasted_iota(jnp.int32, (blk, blk), 1) // CHUNK
    visible = key_chunk <= query_chunk
    ones = jnp.ones((ONES_ROWS, blk), _BF16)

    heads = range(MLA_HEADS)

    def head_scores(ki):
        rows = pl.ds(pl.multiple_of(ki * blk, blk), blk)
        k_rope = kr_ref[0, rows, :]

        def score(h):
            qt = qt_ref[0, 0, h * MLA_QK_PAD:(h + 1) * MLA_QK_PAD, :]
            k = jnp.concatenate([kn_ref[0, rows, h * MLA_NOPE:(h + 1) * MLA_NOPE], k_rope], axis=-1)
            return _dot(k, qt)

        scores = [score(h) for h in heads[:SCORE_LOOKAHEAD]]
        for h in heads:
            if h + SCORE_LOOKAHEAD < MLA_HEADS:
                scores.append(score(h + SCORE_LOOKAHEAD))
            yield h, scores[h]

    def values(ki, h):
        return jnp.concatenate([vt_ref[0, ki, h * MLA_V:(h + 1) * MLA_V, :], ones], axis=0)

    def running_max_step(ki, carry=None):
        seen = jnp.logical_or(visible, ki < qi)
        for h, st in head_scores(ki):
            st = jnp.where(seen, st, -jnp.inf)
            m_prev = m_s[h]
            m_new = jnp.maximum(m_prev, jnp.max(st, axis=0, keepdims=True))
            pt = jnp.exp2(st - m_new).astype(_BF16)
            alpha = jnp.exp2(m_prev - m_new)
            acc_s[h] = alpha * acc_s[h] + _dot(values(ki, h), pt)
            m_s[h] = m_new
        return carry

    def fixed_reference_first_step():
        seen = jnp.logical_or(visible, qi > 0)
        for h, st in head_scores(0):
            st = jnp.where(seen, st, -jnp.inf)
            m_ref = jnp.max(st[0:CHUNK], axis=0, keepdims=True)
            m_s[h] = m_ref
            g_s[h] = jnp.max(st, axis=0, keepdims=True)
            pt = jnp.exp2(st - m_ref).astype(_BF16)
            acc_s[h] = _dot(values(0, h), pt)

    def finish_head(h, acc):
        head = slice(h * MLA_V, (h + 1) * MLA_V)
        o = (acc[0:MLA_V, :] / acc[MLA_V:MLA_V + 1, :]).T
        y_s[:, head] = (sg_ref[0, :, head].astype(_F32) * o).astype(_BF16)

    def finish():
        for h in heads:
            finish_head(h, acc_s[h])

    def fixed_reference_step(ki, last):
        for h, st in head_scores(ki):
            if last:
                st = jnp.where(visible, st, -jnp.inf)
            g_s[h] = jnp.maximum(g_s[h], jnp.max(st, axis=0, keepdims=True))
            pt = jnp.exp2(st - m_s[h]).astype(_BF16)
            acc = acc_s[h] + _dot(values(ki, h), pt)
            if last:
                finish_head(h, acc)
            else:
                acc_s[h] = acc

    def reset():
        m_s[...] = jnp.full_like(m_s, -jnp.inf)
        acc_s[...] = jnp.zeros_like(acc_s)

    fixed_reference_first_step()
    lax.fori_loop(1, qi, lambda ki, c: (fixed_reference_step(ki, last=False), c)[1], 0)

    @pl.when(qi > 0)
    def _():
        fixed_reference_step(qi, last=True)

    @pl.when(qi == 0)
    def _():
        finish()

    in_range = jnp.max(g_s[...] - m_s[...]) <= EXP2_RANGE

    @pl.when(jnp.logical_not(in_range))
    def _():
        reset()
        lax.fori_loop(0, qi + 1, running_max_step, 0)
        finish()

    y = _dot(y_s[...], wout_ref[...])
    o_ref[0] = h_ref[0] + y * _rms_scale(y) * gpost_ref[...]


def _attention_layer(h, qt, kn, kr, vt, sg, w_out, g_post):
    b, s, d = h.shape
    blk = ATTN_BLOCK
    assert vt.shape == (b, s // blk, MLA_WIDTH, blk)
    assert qt.shape == (b, s // blk, MLA_HEADS * MLA_QK_PAD, blk)
    whole = lambda i, j: (i, 0, 0)
    qblk = lambda i, j: (i, j, 0)
    const = lambda i, j: (0, 0)
    return pl.pallas_call(
        _attention_kernel,
        out_shape=jax.ShapeDtypeStruct((b, s, d), _F32),
        grid=(b, s // blk),
        in_specs=[
            pl.BlockSpec((1, 1) + qt.shape[2:], lambda i, j: (i, j, 0, 0)),
            pl.BlockSpec((1, s, kn.shape[-1]), whole),
            pl.BlockSpec((1, s, kr.shape[-1]), whole),
            pl.BlockSpec((1,) + vt.shape[1:], lambda i, j: (i, 0, 0, 0)),
            pl.BlockSpec((1, blk, MLA_WIDTH), qblk),
            pl.BlockSpec((1, blk, d), qblk),
            _single_buffered(w_out.shape, const),
            _single_buffered((1, d), const),
        ],
        out_specs=pl.BlockSpec((1, blk, d), qblk),
        scratch_shapes=[
            pltpu.VMEM((MLA_HEADS, 1, blk), _F32),
            pltpu.VMEM((MLA_HEADS, 1, blk), _F32),
            pltpu.VMEM((MLA_HEADS, MLA_V + ONES_ROWS, blk), _F32),
            pltpu.VMEM((blk, MLA_WIDTH), _BF16),
        ],
        compiler_params=pltpu.CompilerParams(
            dimension_semantics=("parallel", "arbitrary"),
            vmem_limit_bytes=VMEM_LIMIT_BYTES),
        name="latent_attention",
    )(qt, kn, kr, vt, sg, h, w_out.astype(_BF16), g_post.reshape(1, d))


@jax.jit
def kernel(x, g_pre_a, w_in_a, gn_gain_a, w_out_a, g_post_a, g_kv, w_kv_a, g_kv_lat, w_uk, w_uv,
           g_pre_b, w_in_b, g_q_lat, w_uq, w_out_b, g_post_b):
    assert w_in_a.shape[0] == 1 and w_in_b.shape[0] == 1
    h = _retention_layer(x, g_pre_a[0], w_in_a[0], gn_gain_a[0], w_out_a[0], g_post_a[0])
    qt, kn, kr, vt, sg = _projections(h, g_kv, w_kv_a, g_kv_lat, w_uk, w_uv,
                                      g_pre_b[0], w_in_b[0], g_q_lat[0], w_uq[0])
    return _attention_layer(h, qt, kn, kr, vt, sg, w_out_b[0], g_post_b[0])
```

```python
import jax
import jax.numpy as jnp
import numpy as np
from jax import lax
from jax.experimental import pallas as pl
from jax.experimental.pallas import tpu as pltpu

D_MODEL = 1024
CHUNK = 64
RMS_EPS = 1e-6
ROPE_BASE = 10000.0

RET_HEADS = 8
RET_QK = 128
RET_V = 256
RET_QK_WIDTH = RET_HEADS * RET_QK
RET_WIDTH = RET_HEADS * RET_V

MLA_HEADS = 8
MLA_NOPE = 128
MLA_ROPE = 64
MLA_V = 128
MLA_WIDTH = MLA_HEADS * MLA_V
Q_LORA = 384
KV_LORA = 256

LANES = 128
MLA_QK_PAD = 2 * LANES
VMEM_BYTES_V7X = 64 * 1024 * 1024
VMEM_LIMIT_BYTES = VMEM_BYTES_V7X - 4 * 1024 * 1024

RET_TOKEN_BLOCK = 512
RET_SPAN = 256
PROJ_TOKEN_BLOCK = 512
ATTN_BLOCK = PROJ_TOKEN_BLOCK
SCORE_LOOKAHEAD = 2
ONES_ROWS = 16
EXP2_RANGE = 16.0

LOG2_E = 1.4426950408889634

_BF16 = jnp.bfloat16
_F32 = jnp.float32


def _single_buffered(shape, index_map):
    return pl.BlockSpec(shape, index_map, pipeline_mode=pl.Buffered(1))


def _rms_scale(x):
    return lax.rsqrt(jnp.mean(x * x, axis=-1, keepdims=True) + RMS_EPS)


def _rotate(x, cos, sin_signed):
    return x * cos + pltpu.roll(x, LANES // 2, axis=1) * sin_signed


def _dot(a, b):
    return jnp.dot(a, b, preferred_element_type=_F32)


def _dot_nt(a, b):
    return lax.dot_general(a, b, (((1,), (1,)), ((), ())), preferred_element_type=_F32)


def _dot_tn(a, b):
    return lax.dot_general(a, b, (((0,), (0,)), ((), ())), preferred_element_type=_F32)


def _retention_kernel(x_ref, gpre_ref, win_ref, cos_ref, sin_ref, dmask_ref, xi_ref, zeta_ref,
                      gspan_ref, gn_ref, wout_ref, gpost_ref, o_ref,
                      state_ref, q_s, k_s, v_s, gate_s, y_s):
    @pl.when(pl.program_id(1) == 0)
    def _():
        state_ref[...] = jnp.zeros_like(state_ref)

    x = x_ref[0]
    hb = (x * _rms_scale(x) * gpre_ref[...]).astype(_BF16)
    cos = cos_ref[...]
    sin = sin_ref[...]

    q_all = _dot(hb, win_ref[:, 0:RET_QK_WIDTH])
    for h in range(RET_HEADS):
        q_s[h] = _rotate(q_all[:, h * RET_QK:(h + 1) * RET_QK], cos, sin)
    k_all = _dot(hb, win_ref[:, RET_QK_WIDTH:2 * RET_QK_WIDTH])
    for h in range(RET_HEADS):
        k_s[h] = _rotate(k_all[:, h * RET_QK:(h + 1) * RET_QK], cos, sin)
    v_s[...] = _dot(hb, win_ref[:, 2 * RET_QK_WIDTH:2 * RET_QK_WIDTH + RET_WIDTH]).astype(_BF16)
    gate_s[...] = _dot(hb, win_ref[:, 2 * RET_QK_WIDTH + RET_WIDTH:])

    for s in range(RET_TOKEN_BLOCK // RET_SPAN):
        rows = slice(s * RET_SPAN, (s + 1) * RET_SPAN)
        for h in range(RET_HEADS):
            cols = slice(h * RET_V, (h + 1) * RET_V)
            q = q_s[h, rows, :]
            k = k_s[h, rows, :]
            v = v_s[rows, cols]
            scores = _dot_nt(q.astype(_BF16), k.astype(_BF16)) * dmask_ref[h]
            state = state_ref[h]
            o = _dot(scores.astype(_BF16), v)
            o = o + _dot((q * xi_ref[h]).astype(_BF16), state.astype(_BF16))
            state_ref[h] = state * gspan_ref[h] + _dot_tn((k * zeta_ref[h]).astype(_BF16), v)
            mu = jnp.mean(o, axis=-1, keepdims=True)
            oc = o - mu
            var = jnp.mean(oc * oc, axis=-1, keepdims=True)
            on = oc * lax.rsqrt(var + RMS_EPS) * gn_ref[:, cols]
            gate = gate_s[rows, cols]
            silu = gate / (1.0 + jnp.exp(-gate))
            y_s[rows, cols] = (silu * on).astype(_BF16)

    y = _dot(y_s[...], wout_ref[...])
    o_ref[0] = x + y * _rms_scale(y) * gpost_ref[...]


def _retention_tables():
    log_g = np.log(1.0 - np.exp2(-5.0 - np.arange(RET_HEADS, dtype=np.float64)))
    idx = np.arange(RET_SPAN, dtype=np.float64)
    diff = idx[:, None] - idx[None, :]
    chunk = np.arange(RET_SPAN) // CHUNK
    same = chunk[:, None] == chunk[None, :]
    earlier = chunk[None, :] < chunk[:, None]
    dist = np.where(same, np.abs(diff), diff)
    decay = np.exp(log_g[:, None, None] * np.where(same | earlier, dist, 0.0)[None])
    k_scale = RET_QK ** -0.5
    dmask = np.where((same | earlier)[None], decay, 0.0) * k_scale
    xi = np.exp(log_g[:, None] * (idx + 1.0)[None]) * k_scale
    zeta = np.exp(log_g[:, None] * (RET_SPAN - 1.0 - idx)[None])
    g_span = np.exp(log_g * RET_SPAN)
    xi = np.broadcast_to(xi[:, :, None], (RET_HEADS, RET_SPAN, RET_QK))
    zeta = np.broadcast_to(zeta[:, :, None], (RET_HEADS, RET_SPAN, RET_QK))
    g_span = np.broadcast_to(g_span[:, None, None], (RET_HEADS, 1, RET_V))
    return tuple(np.ascontiguousarray(t, dtype=np.float32) for t in (dmask, xi, zeta, g_span))


def _rope_angles(seq, half):
    inv = ROPE_BASE ** (-np.arange(half, dtype=np.float64) / half)
    ang = np.arange(seq, dtype=np.float64)[:, None] * inv[None, :]
    return np.cos(ang).astype(np.float32), np.sin(ang).astype(np.float32)


def _retention_layer(x, g_pre, w_in, gn_gain, w_out, g_post):
    b, s, d = x.shape
    tb = RET_TOKEN_BLOCK
    cos, sin = _rope_angles(s, RET_QK // 2)
    cos_t = np.concatenate([cos, cos], axis=-1)
    sin_t = np.concatenate([-sin, sin], axis=-1)
    dmask, xi, zeta, g_span = _retention_tables()
    const2 = lambda i, j: (0, 0)
    const3 = lambda i, j: (0, 0, 0)
    return pl.pallas_call(
        _retention_kernel,
        out_shape=jax.ShapeDtypeStruct((b, s, d), _F32),
        grid=(b, s // tb),
        in_specs=[
            pl.BlockSpec((1, tb, d), lambda i, j: (i, j, 0)),
            _single_buffered((1, d), const2),
            _single_buffered(w_in.shape, const2),
            pl.BlockSpec((tb, RET_QK), lambda i, j: (j, 0)),
            pl.BlockSpec((tb, RET_QK), lambda i, j: (j, 0)),
            _single_buffered(dmask.shape, const3),
            _single_buffered(xi.shape, const3),
            _single_buffered(zeta.shape, const3),
            _single_buffered(g_span.shape, const3),
            _single_buffered((1, RET_WIDTH), const2),
            _single_buffered(w_out.shape, const2),
            _single_buffered((1, d), const2),
        ],
        out_specs=pl.BlockSpec((1, tb, d), lambda i, j: (i, j, 0)),
        scratch_shapes=[
            pltpu.VMEM((RET_HEADS, RET_QK, RET_V), _F32),
            pltpu.VMEM((RET_HEADS, tb, RET_QK), _F32),
            pltpu.VMEM((RET_HEADS, tb, RET_QK), _F32),
            pltpu.VMEM((tb, RET_WIDTH), _BF16),
            pltpu.VMEM((tb, RET_WIDTH), _F32),
            pltpu.VMEM((tb, RET_WIDTH), _BF16),
        ],
        compiler_params=pltpu.CompilerParams(
            dimension_semantics=("parallel", "arbitrary"),
            vmem_limit_bytes=VMEM_LIMIT_BYTES),
        name="retention_layer",
    )(x, g_pre.reshape(1, d), w_in.astype(_BF16), cos_t, sin_t, dmask, xi, zeta, g_span,
      gn_gain.reshape(1, RET_WIDTH), w_out.astype(_BF16), g_post.reshape(1, d))


def _projection_kernel(h_ref, gkv_ref, wkva_ref, gkvlat_ref, wuk_ref, wuvt_ref, gpre_ref, winb_ref,
                       gqlat_ref, wuqt_ref, cos_ref, sin_ref, cost_ref, sint_ref,
                       qt_ref, kn_ref, kr_ref, vt_ref, sg_ref):
    x = h_ref[...]
    xn = x * _rms_scale(x)
    cos = cos_ref[...]
    sin = sin_ref[...]

    a = _dot((xn * gkv_ref[...]).astype(_BF16), wkva_ref[...])
    c_kv = a[:, :KV_LORA]
    kr_ref[...] = _rotate(a[:, KV_LORA:], cos, sin).astype(_BF16)
    c_kv = (c_kv * _rms_scale(c_kv) * gkvlat_ref[...]).astype(_BF16)
    kn_ref[...] = _dot(c_kv, wuk_ref[...]).astype(_BF16)
    vt_ref[0] = _dot_nt(wuvt_ref[...], c_kv).astype(_BF16)

    proj = _dot((xn * gpre_ref[...]).astype(_BF16), winb_ref[...])
    gate = proj[:, Q_LORA:]
    sg_ref[...] = (gate / (1.0 + jnp.exp(-gate))).astype(_BF16)
    c_q = proj[:, :Q_LORA]
    c_q = (c_q * _rms_scale(c_q) * gqlat_ref[...]).astype(_BF16)
    qt = _dot_nt(wuqt_ref[...], c_q)
    scale = (MLA_NOPE + MLA_ROPE) ** -0.5 * LOG2_E
    half = MLA_ROPE // 2
    cos_rows = cost_ref[...]
    sin_rows = sint_ref[...]
    zero_rows = jnp.zeros((LANES // 2 - half, qt.shape[1]), _BF16)
    for h in range(MLA_HEADS):
        src = h * (MLA_NOPE + MLA_ROPE)
        dst = h * MLA_QK_PAD
        qt_ref[0, dst:dst + MLA_NOPE, :] = (qt[src:src + MLA_NOPE] * scale).astype(_BF16)
        x1 = qt[src + MLA_NOPE:src + MLA_NOPE + half]
        x2 = qt[src + MLA_NOPE + half:src + MLA_NOPE + 2 * half]
        dst += MLA_NOPE
        qt_ref[0, dst:dst + half, :] = ((x1 * cos_rows - x2 * sin_rows) * scale).astype(_BF16)
        qt_ref[0, dst + half:dst + LANES // 2, :] = zero_rows
        dst += LANES // 2
        qt_ref[0, dst:dst + half, :] = ((x1 * sin_rows + x2 * cos_rows) * scale).astype(_BF16)
        qt_ref[0, dst + half:dst + LANES // 2, :] = zero_rows


def _spread_rope_columns(w):
    half = MLA_ROPE // 2
    z = jnp.zeros(w.shape[:-1] + (LANES // 2 - half,), w.dtype)
    return jnp.concatenate([w[..., :half], z, w[..., half:], z], axis=-1)


def _projections(h, g_kv, w_kv_a, g_kv_lat, w_uk, w_uv, g_pre, w_in, g_q_lat, w_uq):
    b, s, d = h.shape
    t = b * s
    tb = PROJ_TOKEN_BLOCK
    half = MLA_ROPE // 2
    cos, sin = _rope_angles(s, half)
    ones = np.ones((s, LANES // 2 - half), np.float32)
    zeros = np.zeros((s, LANES // 2 - half), np.float32)
    cos_t = np.concatenate([cos, ones, cos, ones], axis=-1)
    sin_t = np.concatenate([-sin, zeros, sin, zeros], axis=-1)

    wkva = jnp.concatenate([w_kv_a[:, :KV_LORA], _spread_rope_columns(w_kv_a[:, KV_LORA:])], axis=-1)

    const = lambda i: (0, 0)
    tok = lambda i: (i, 0)
    pos = lambda i: (i % (s // tb), 0)
    pos_t = lambda i: (0, i % (s // tb))
    weights = [
        g_kv.reshape(1, d), wkva.astype(_BF16), g_kv_lat.reshape(1, KV_LORA), w_uk.astype(_BF16),
        w_uv.T.astype(_BF16), g_pre.reshape(1, d), w_in.astype(_BF16), g_q_lat.reshape(1, Q_LORA),
        w_uq.T.astype(_BF16),
    ]

    def token_major(width):
        return jax.ShapeDtypeStruct((t, width), _BF16), pl.BlockSpec((tb, width), tok)

    def feature_major(width):
        return (jax.ShapeDtypeStruct((t // tb, width, tb), _BF16),
                pl.BlockSpec((1, width, tb), lambda i: (i, 0, 0)))

    outs = [feature_major(MLA_HEADS * MLA_QK_PAD), token_major(MLA_HEADS * MLA_NOPE),
            token_major(LANES), feature_major(MLA_WIDTH), token_major(MLA_WIDTH)]
    qt, kn, kr, vt, sg = pl.pallas_call(
        _projection_kernel,
        out_shape=[o[0] for o in outs],
        grid=(t // tb,),
        in_specs=[pl.BlockSpec((tb, d), tok)]
        + [_single_buffered(w.shape, const) for w in weights]
        + [pl.BlockSpec((tb, LANES), pos), pl.BlockSpec((tb, LANES), pos),
           pl.BlockSpec((half, tb), pos_t), pl.BlockSpec((half, tb), pos_t)],
        out_specs=[o[1] for o in outs],
        compiler_params=pltpu.CompilerParams(
            dimension_semantics=("parallel",),
            vmem_limit_bytes=VMEM_LIMIT_BYTES),
        name="latent_projections",
    )(h.reshape(t, d), *weights, cos_t, sin_t,
      np.ascontiguousarray(cos.T), np.ascontiguousarray(sin.T))
    kn, kr, sg = [o.reshape(b, s, -1) for o in (kn, kr, sg)]
    return (qt.reshape(b, s // tb, MLA_HEADS * MLA_QK_PAD, tb), kn, kr,
            vt.reshape(b, s // tb, MLA_WIDTH, tb), sg)


def _attention_kernel(qt_ref, kn_ref, kr_ref, vt_ref, sg_ref, h_ref, wout_ref, gpost_ref, o_ref,
                      m_s, g_s, acc_s, y_s):
    blk = ATTN_BLOCK
    qi = pl.program_id(1)
    key_chunk = lax.broadcasted_iota(jnp.int32, (blk, blk), 0) // CHUNK
    query_chunk = lax.broadcasted_iota(jnp.int32, (blk, blk), 1) // CHUNK
    visible = key_chunk <= query_chunk
    ones = jnp.ones((ONES_ROWS, blk), _BF16)

    heads = range(MLA_HEADS)

    def head_scores(ki):
        rows = pl.ds(pl.multiple_of(ki * blk, blk), blk)
        k_rope = kr_ref[0, rows, :]

        def score(h):
            qt = qt_ref[0, 0, h * MLA_QK_PAD:(h + 1) * MLA_QK_PAD, :]
            k = jnp.concatenate([kn_ref[0, rows, h * MLA_NOPE:(h + 1) * MLA_NOPE], k_rope], axis=-1)
            return _dot(k, qt)

        scores = [score(h) for h in heads[:SCORE_LOOKAHEAD]]
        for h in heads:
            if h + SCORE_LOOKAHEAD < MLA_HEADS:
                scores.append(score(h + SCORE_LOOKAHEAD))
            yield h, scores[h]

    def values(ki, h):
        return jnp.concatenate([vt_ref[0, ki, h * MLA_V:(h + 1) * MLA_V, :], ones], axis=0)

    def running_max_step(ki, carry=None):
        seen = jnp.logical_or(visible, ki < qi)
        for h, st in head_scores(ki):
            st = jnp.where(seen, st, -jnp.inf)
            m_prev = m_s[h]
            m_new = jnp.maximum(m_prev, jnp.max(st, axis=0, keepdims=True))
            pt = jnp.exp2(st - m_new).astype(_BF16)
            alpha = jnp.exp2(m_prev - m_new)
            acc_s[h] = alpha * acc_s[h] + _dot(values(ki, h), pt)
            m_s[h] = m_new
        return carry

    def fixed_reference_first_step(last):
        for h, st in head_scores(0):
            if last:
                st = jnp.where(visible, st, -jnp.inf)
            m_ref = jnp.max(st[0:CHUNK], axis=0, keepdims=True)
            m_s[h] = m_ref
            g_s[h] = jnp.max(st, axis=0, keepdims=True)
            pt = jnp.exp2(st - m_ref).astype(_BF16)
            acc = _dot(values(0, h), pt)
            if last:
                finish_head(h, acc)
            else:
                acc_s[h] = acc

    def finish_head(h, acc):
        head = slice(h * MLA_V, (h + 1) * MLA_V)
        o = (acc[0:MLA_V, :] / acc[MLA_V:MLA_V + 1, :]).T
        y_s[:, head] = (sg_ref[0, :, head].astype(_F32) * o).astype(_BF16)

    def finish():
        for h in heads:
            finish_head(h, acc_s[h])

    def fixed_reference_step(ki, last):
        for h, st in head_scores(ki):
            if last:
                st = jnp.where(visible, st, -jnp.inf)
            g_s[h] = jnp.maximum(g_s[h], jnp.max(st, axis=0, keepdims=True))
            pt = jnp.exp2(st - m_s[h]).astype(_BF16)
            acc = acc_s[h] + _dot(values(ki, h), pt)
            if last:
                finish_head(h, acc)
            else:
                acc_s[h] = acc

    def reset():
        m_s[...] = jnp.full_like(m_s, -jnp.inf)
        acc_s[...] = jnp.zeros_like(acc_s)

    @pl.when(qi == 0)
    def _():
        fixed_reference_first_step(last=True)

    @pl.when(qi > 0)
    def _():
        fixed_reference_first_step(last=False)
        lax.fori_loop(1, qi, lambda ki, c: (fixed_reference_step(ki, last=False), c)[1], 0)
        fixed_reference_step(qi, last=True)

    in_range = jnp.max(g_s[...] - m_s[...]) <= EXP2_RANGE

    @pl.when(jnp.logical_not(in_range))
    def _():
        reset()
        lax.fori_loop(0, qi + 1, running_max_step, 0)
        finish()

    y = _dot(y_s[...], wout_ref[...])
    o_ref[0] = h_ref[0] + y * _rms_scale(y) * gpost_ref[...]


def _attention_layer(h, qt, kn, kr, vt, sg, w_out, g_post):
    b, s, d = h.shape
    blk = ATTN_BLOCK
    assert vt.shape == (b, s // blk, MLA_WIDTH, blk)
    assert qt.shape == (b, s // blk, MLA_HEADS * MLA_QK_PAD, blk)
    whole = lambda i, j: (i, 0, 0)
    qblk = lambda i, j: (i, j, 0)
    const = lambda i, j: (0, 0)
    return pl.pallas_call(
        _attention_kernel,
        out_shape=jax.ShapeDtypeStruct((b, s, d), _F32),
        grid=(b, s // blk),
        in_specs=[
            pl.BlockSpec((1, 1) + qt.shape[2:], lambda i, j: (i, j, 0, 0)),
            pl.BlockSpec((1, s, kn.shape[-1]), whole),
            pl.BlockSpec((1, s, kr.shape[-1]), whole),
            pl.BlockSpec((1,) + vt.shape[1:], lambda i, j: (i, 0, 0, 0)),
            pl.BlockSpec((1, blk, MLA_WIDTH), qblk),
            pl.BlockSpec((1, blk, d), qblk),
            _single_buffered(w_out.shape, const),
            _single_buffered((1, d), const),
        ],
        out_specs=pl.BlockSpec((1, blk, d), qblk),
        scratch_shapes=[
            pltpu.VMEM((MLA_HEADS, 1, blk), _F32),
            pltpu.VMEM((MLA_HEADS, 1, blk), _F32),
            pltpu.VMEM((MLA_HEADS, MLA_V + ONES_ROWS, blk), _F32),
            pltpu.VMEM((blk, MLA_WIDTH), _BF16),
        ],
        compiler_params=pltpu.CompilerParams(
            dimension_semantics=("parallel", "arbitrary"),
            vmem_limit_bytes=VMEM_LIMIT_BYTES),
        name="latent_attention",
    )(qt, kn, kr, vt, sg, h, w_out.astype(_BF16), g_post.reshape(1, d))


@jax.jit
def kernel(x, g_pre_a, w_in_a, gn_gain_a, w_out_a, g_post_a, g_kv, w_kv_a, g_kv_lat, w_uk, w_uv,
           g_pre_b, w_in_b, g_q_lat, w_uq, w_out_b, g_post_b):
    assert w_in_a.shape[0] == 1 and w_in_b.shape[0] == 1
    h = _retention_layer(x, g_pre_a[0], w_in_a[0], gn_gain_a[0], w_out_a[0], g_post_a[0])
    qt, kn, kr, vt, sg = _projections(h, g_kv, w_kv_a, g_kv_lat, w_uk, w_uv,
                                      g_pre_b[0], w_in_b[0], g_q_lat[0], w_uq[0])
    return _attention_layer(h, qt, kn, kr, vt, sg, w_out_b[0], g_post_b[0])
```

```python
import jax
import jax.numpy as jnp
import numpy as np
from jax import lax
from jax.experimental import pallas as pl
from jax.experimental.pallas import tpu as pltpu

D_MODEL = 1024
CHUNK = 64
RMS_EPS = 1e-6
ROPE_BASE = 10000.0

RET_HEADS = 8
RET_QK = 128
RET_V = 256
RET_QK_WIDTH = RET_HEADS * RET_QK
RET_WIDTH = RET_HEADS * RET_V

MLA_HEADS = 8
MLA_NOPE = 128
MLA_ROPE = 64
MLA_V = 128
MLA_WIDTH = MLA_HEADS * MLA_V
Q_LORA = 384
KV_LORA = 256

LANES = 128
MLA_QK_PAD = 2 * LANES
VMEM_BYTES_V7X = 64 * 1024 * 1024
VMEM_LIMIT_BYTES = VMEM_BYTES_V7X - 4 * 1024 * 1024

RET_TOKEN_BLOCK = 512
RET_SPAN = 256
PROJ_TOKEN_BLOCK = 512
ATTN_BLOCK = PROJ_TOKEN_BLOCK
SCORE_LOOKAHEAD = 2
ONES_ROWS = 16
EXP2_RANGE = 16.0

LOG2_E = 1.4426950408889634

_BF16 = jnp.bfloat16
_F32 = jnp.float32


def _single_buffered(shape, index_map):
    return pl.BlockSpec(shape, index_map, pipeline_mode=pl.Buffered(1))


def _rms_scale(x):
    return lax.rsqrt(jnp.mean(x * x, axis=-1, keepdims=True) + RMS_EPS)


def _rotate(x, cos, sin_signed):
    return x * cos + pltpu.roll(x, LANES // 2, axis=1) * sin_signed


def _dot(a, b):
    return jnp.dot(a, b, preferred_element_type=_F32)


def _dot_nt(a, b):
    return lax.dot_general(a, b, (((1,), (1,)), ((), ())), preferred_element_type=_F32)


def _dot_tn(a, b):
    return lax.dot_general(a, b, (((0,), (0,)), ((), ())), preferred_element_type=_F32)


def _retention_kernel(x_ref, gpre_ref, win_ref, cos_ref, sin_ref, dmask_ref, xi_ref, zeta_ref,
                      gspan_ref, gn_ref, wout_ref, gpost_ref, o_ref,
                      state_ref, q_s, k_s, v_s, gate_s, y_s):
    @pl.when(pl.program_id(1) == 0)
    def _():
        state_ref[...] = jnp.zeros_like(state_ref)

    x = x_ref[0]
    xg = x * gpre_ref[...]
    r = _rms_scale(x)
    xb = xg.astype(_BF16)
    cos_q = cos_ref[...] * r
    sin_q = sin_ref[...] * r
    cos_k = cos_q * r
    sin_k = sin_q * r

    q_all = _dot(xb, win_ref[:, 0:RET_QK_WIDTH])
    for h in range(RET_HEADS):
        q_s[h] = _rotate(q_all[:, h * RET_QK:(h + 1) * RET_QK], cos_q, sin_q)
    k_all = _dot(xb, win_ref[:, RET_QK_WIDTH:2 * RET_QK_WIDTH])
    for h in range(RET_HEADS):
        k_s[h] = _rotate(k_all[:, h * RET_QK:(h + 1) * RET_QK], cos_k, sin_k)
    v_s[...] = _dot(xb, win_ref[:, 2 * RET_QK_WIDTH:2 * RET_QK_WIDTH + RET_WIDTH]).astype(_BF16)
    hb = (xg * r).astype(_BF16)
    gate_s[...] = _dot(hb, win_ref[:, 2 * RET_QK_WIDTH + RET_WIDTH:])

    for s in range(RET_TOKEN_BLOCK // RET_SPAN):
        rows = slice(s * RET_SPAN, (s + 1) * RET_SPAN)
        for h in range(RET_HEADS):
            cols = slice(h * RET_V, (h + 1) * RET_V)
            q = q_s[h, rows, :]
            k = k_s[h, rows, :]
            v = v_s[rows, cols]
            scores = _dot_nt(q.astype(_BF16), k.astype(_BF16)) * dmask_ref[h]
            state = state_ref[h]
            o = _dot(scores.astype(_BF16), v)
            o = o + _dot((q * xi_ref[h]).astype(_BF16), state.astype(_BF16))
            state_ref[h] = state * gspan_ref[h] + _dot_tn((k * zeta_ref[h]).astype(_BF16), v)
            mu = jnp.mean(o, axis=-1, keepdims=True)
            oc = o - mu
            var = jnp.mean(oc * oc, axis=-1, keepdims=True)
            on = oc * lax.rsqrt(var + RMS_EPS) * gn_ref[:, cols]
            gate = gate_s[rows, cols]
            silu = gate / (1.0 + jnp.exp(-gate))
            y_s[rows, cols] = (silu * on).astype(_BF16)

    y = _dot(y_s[...], wout_ref[...])
    o_ref[0] = x + y * _rms_scale(y) * gpost_ref[...]


def _retention_tables():
    log_g = np.log(1.0 - np.exp2(-5.0 - np.arange(RET_HEADS, dtype=np.float64)))
    idx = np.arange(RET_SPAN, dtype=np.float64)
    diff = idx[:, None] - idx[None, :]
    chunk = np.arange(RET_SPAN) // CHUNK
    same = chunk[:, None] == chunk[None, :]
    earlier = chunk[None, :] < chunk[:, None]
    dist = np.where(same, np.abs(diff), diff)
    decay = np.exp(log_g[:, None, None] * np.where(same | earlier, dist, 0.0)[None])
    k_scale = RET_QK ** -0.5
    dmask = np.where((same | earlier)[None], decay, 0.0) * k_scale
    xi = np.exp(log_g[:, None] * (idx + 1.0)[None]) * k_scale
    zeta = np.exp(log_g[:, None] * (RET_SPAN - 1.0 - idx)[None])
    g_span = np.exp(log_g * RET_SPAN)
    xi = np.broadcast_to(xi[:, :, None], (RET_HEADS, RET_SPAN, RET_QK))
    zeta = np.broadcast_to(zeta[:, :, None], (RET_HEADS, RET_SPAN, RET_QK))
    g_span = np.broadcast_to(g_span[:, None, None], (RET_HEADS, 1, RET_V))
    return tuple(np.ascontiguousarray(t, dtype=np.float32) for t in (dmask, xi, zeta, g_span))


def _rope_angles(seq, half):
    inv = ROPE_BASE ** (-np.arange(half, dtype=np.float64) / half)
    ang = np.arange(seq, dtype=np.float64)[:, None] * inv[None, :]
    return np.cos(ang).astype(np.float32), np.sin(ang).astype(np.float32)


def _retention_layer(x, g_pre, w_in, gn_gain, w_out, g_post):
    b, s, d = x.shape
    tb = RET_TOKEN_BLOCK
    cos, sin = _rope_angles(s, RET_QK // 2)
    cos_t = np.concatenate([cos, cos], axis=-1)
    sin_t = np.concatenate([-sin, sin], axis=-1)
    dmask, xi, zeta, g_span = _retention_tables()
    const2 = lambda i, j: (0, 0)
    const3 = lambda i, j: (0, 0, 0)
    return pl.pallas_call(
        _retention_kernel,
        out_shape=jax.ShapeDtypeStruct((b, s, d), _F32),
        grid=(b, s // tb),
        in_specs=[
            pl.BlockSpec((1, tb, d), lambda i, j: (i, j, 0)),
            _single_buffered((1, d), const2),
            _single_buffered(w_in.shape, const2),
            pl.BlockSpec((tb, RET_QK), lambda i, j: (j, 0)),
            pl.BlockSpec((tb, RET_QK), lambda i, j: (j, 0)),
            _single_buffered(dmask.shape, const3),
            _single_buffered(xi.shape, const3),
            _single_buffered(zeta.shape, const3),
            _single_buffered(g_span.shape, const3),
            _single_buffered((1, RET_WIDTH), const2),
            _single_buffered(w_out.shape, const2),
            _single_buffered((1, d), const2),
        ],
        out_specs=pl.BlockSpec((1, tb, d), lambda i, j: (i, j, 0)),
        scratch_shapes=[
            pltpu.VMEM((RET_HEADS, RET_QK, RET_V), _F32),
            pltpu.VMEM((RET_HEADS, tb, RET_QK), _F32),
            pltpu.VMEM((RET_HEADS, tb, RET_QK), _F32),
            pltpu.VMEM((tb, RET_WIDTH), _BF16),
            pltpu.VMEM((tb, RET_WIDTH), _F32),
            pltpu.VMEM((tb, RET_WIDTH), _BF16),
        ],
        compiler_params=pltpu.CompilerParams(
            dimension_semantics=("parallel", "arbitrary"),
            vmem_limit_bytes=VMEM_LIMIT_BYTES),
        name="retention_layer",
    )(x, g_pre.reshape(1, d), w_in.astype(_BF16), cos_t, sin_t, dmask, xi, zeta, g_span,
      gn_gain.reshape(1, RET_WIDTH), w_out.astype(_BF16), g_post.reshape(1, d))


def _projection_kernel(h_ref, gkv_ref, wkva_ref, gkvlat_ref, wuk_ref, wuvt_ref, gpre_ref, winb_ref,
                       gqlat_ref, wuqt_ref, cos_ref, sin_ref, cost_ref, sint_ref,
                       qt_ref, kn_ref, kr_ref, vt_ref, sg_ref):
    x = h_ref[...]
    xn = x * _rms_scale(x)
    cos = cos_ref[...]
    sin = sin_ref[...]

    a = _dot((xn * gkv_ref[...]).astype(_BF16), wkva_ref[...])
    c_kv = a[:, :KV_LORA]
    kr_ref[...] = _rotate(a[:, KV_LORA:], cos, sin).astype(_BF16)
    c_kv = (c_kv * _rms_scale(c_kv) * gkvlat_ref[...]).astype(_BF16)
    kn_ref[...] = _dot(c_kv, wuk_ref[...]).astype(_BF16)
    vt_ref[0] = _dot_nt(wuvt_ref[...], c_kv).astype(_BF16)

    proj = _dot((xn * gpre_ref[...]).astype(_BF16), winb_ref[...])
    gate = proj[:, Q_LORA:]
    sg_ref[...] = (gate / (1.0 + jnp.exp(-gate))).astype(_BF16)
    c_q = proj[:, :Q_LORA]
    c_q = (c_q * _rms_scale(c_q) * gqlat_ref[...]).astype(_BF16)
    qt = _dot_nt(wuqt_ref[...], c_q)
    scale = (MLA_NOPE + MLA_ROPE) ** -0.5 * LOG2_E
    half = MLA_ROPE // 2
    cos_rows = cost_ref[...]
    sin_rows = sint_ref[...]
    zero_rows = jnp.zeros((LANES // 2 - half, qt.shape[1]), _BF16)
    for h in range(MLA_HEADS):
        src = h * (MLA_NOPE + MLA_ROPE)
        dst = h * MLA_QK_PAD
        qt_ref[0, dst:dst + MLA_NOPE, :] = (qt[src:src + MLA_NOPE] * scale).astype(_BF16)
        x1 = qt[src + MLA_NOPE:src + MLA_NOPE + half]
        x2 = qt[src + MLA_NOPE + half:src + MLA_NOPE + 2 * half]
        dst += MLA_NOPE
        qt_ref[0, dst:dst + half, :] = ((x1 * cos_rows - x2 * sin_rows) * scale).astype(_BF16)
        qt_ref[0, dst + half:dst + LANES // 2, :] = zero_rows
        dst += LANES // 2
        qt_ref[0, dst:dst + half, :] = ((x1 * sin_rows + x2 * cos_rows) * scale).astype(_BF16)
        qt_ref[0, dst + half:dst + LANES // 2, :] = zero_rows


def _spread_rope_columns(w):
    half = MLA_ROPE // 2
    z = jnp.zeros(w.shape[:-1] + (LANES // 2 - half,), w.dtype)
    return jnp.concatenate([w[..., :half], z, w[..., half:], z], axis=-1)


def _projections(h, g_kv, w_kv_a, g_kv_lat, w_uk, w_uv, g_pre, w_in, g_q_lat, w_uq):
    b, s, d = h.shape
    t = b * s
    tb = PROJ_TOKEN_BLOCK
    half = MLA_ROPE // 2
    cos, sin = _rope_angles(s, half)
    ones = np.ones((s, LANES // 2 - half), np.float32)
    zeros = np.zeros((s, LANES // 2 - half), np.float32)
    cos_t = np.concatenate([cos, ones, cos, ones], axis=-1)
    sin_t = np.concatenate([-sin, zeros, sin, zeros], axis=-1)

    wkva = jnp.concatenate([w_kv_a[:, :KV_LORA], _spread_rope_columns(w_kv_a[:, KV_LORA:])], axis=-1)

    const = lambda i: (0, 0)
    tok = lambda i: (i, 0)
    pos = lambda i: (i % (s // tb), 0)
    pos_t = lambda i: (0, i % (s // tb))
    weights = [
        g_kv.reshape(1, d), wkva.astype(_BF16), g_kv_lat.reshape(1, KV_LORA), w_uk.astype(_BF16),
        w_uv.T.astype(_BF16), g_pre.reshape(1, d), w_in.astype(_BF16), g_q_lat.reshape(1, Q_LORA),
        w_uq.T.astype(_BF16),
    ]

    def token_major(width):
        return jax.ShapeDtypeStruct((t, width), _BF16), pl.BlockSpec((tb, width), tok)

    def feature_major(width):
        return (jax.ShapeDtypeStruct((t // tb, width, tb), _BF16),
                pl.BlockSpec((1, width, tb), lambda i: (i, 0, 0)))

    outs = [feature_major(MLA_HEADS * MLA_QK_PAD), token_major(MLA_HEADS * MLA_NOPE),
            token_major(LANES), feature_major(MLA_WIDTH), token_major(MLA_WIDTH)]
    qt, kn, kr, vt, sg = pl.pallas_call(
        _projection_kernel,
        out_shape=[o[0] for o in outs],
        grid=(t // tb,),
        in_specs=[pl.BlockSpec((tb, d), tok)]
        + [_single_buffered(w.shape, const) for w in weights]
        + [pl.BlockSpec((tb, LANES), pos), pl.BlockSpec((tb, LANES), pos),
           pl.BlockSpec((half, tb), pos_t), pl.BlockSpec((half, tb), pos_t)],
        out_specs=[o[1] for o in outs],
        compiler_params=pltpu.CompilerParams(
            dimension_semantics=("parallel",),
            vmem_limit_bytes=VMEM_LIMIT_BYTES),
        name="latent_projections",
    )(h.reshape(t, d), *weights, cos_t, sin_t,
      np.ascontiguousarray(cos.T), np.ascontiguousarray(sin.T))
    kn, kr, sg = [o.reshape(b, s, -1) for o in (kn, kr, sg)]
    return (qt.reshape(b, s // tb, MLA_HEADS * MLA_QK_PAD, tb), kn, kr,
            vt.reshape(b, s // tb, MLA_WIDTH, tb), sg)


def _attention_kernel(qt_ref, kn_ref, kr_ref, vt_ref, sg_ref, h_ref, wout_ref, gpost_ref, o_ref,
                      m_s, g_s, acc_s, y_s):
    blk = ATTN_BLOCK
    qi = pl.program_id(1)
    key_chunk = lax.broadcasted_iota(jnp.int32, (blk, blk), 0) // CHUNK
    query_chunk = lax.broadcasted_iota(jnp.int32, (blk, blk), 1) // CHUNK
    visible = key_chunk <= query_chunk
    ones = jnp.ones((ONES_ROWS, blk), _BF16)

    heads = range(MLA_HEADS)

    def head_scores(ki):
        rows = pl.ds(pl.multiple_of(ki * blk, blk), blk)
        k_rope = kr_ref[0, rows, :]

        def score(h):
            qt = qt_ref[0, 0, h * MLA_QK_PAD:(h + 1) * MLA_QK_PAD, :]
            k = jnp.concatenate([kn_ref[0, rows, h * MLA_NOPE:(h + 1) * MLA_NOPE], k_rope], axis=-1)
            return _dot(k, qt)

        scores = [score(h) for h in heads[:SCORE_LOOKAHEAD]]
        for h in heads:
            if h + SCORE_LOOKAHEAD < MLA_HEADS:
                scores.append(score(h + SCORE_LOOKAHEAD))
            yield h, scores[h]

    def values(ki, h):
        return jnp.concatenate([vt_ref[0, ki, h * MLA_V:(h + 1) * MLA_V, :], ones], axis=0)

    def running_max_step(ki, carry=None):
        seen = jnp.logical_or(visible, ki < qi)
        for h, st in head_scores(ki):
            st = jnp.where(seen, st, -jnp.inf)
            m_prev = m_s[h]
            m_new = jnp.maximum(m_prev, jnp.max(st, axis=0, keepdims=True))
            pt = jnp.exp2(st - m_new).astype(_BF16)
            alpha = jnp.exp2(m_prev - m_new)
            acc_s[h] = alpha * acc_s[h] + _dot(values(ki, h), pt)
            m_s[h] = m_new
        return carry

    def fixed_reference_first_step(last):
        for h, st in head_scores(0):
            if last:
                st = jnp.where(visible, st, -jnp.inf)
            m_ref = jnp.max(st[0:CHUNK], axis=0, keepdims=True)
            m_s[h] = m_ref
            g_s[h] = jnp.max(st, axis=0, keepdims=True)
            pt = jnp.exp2(st - m_ref).astype(_BF16)
            acc = _dot(values(0, h), pt)
            if last:
                finish_head(h, acc)
            else:
                acc_s[h] = acc

    def finish_head(h, acc):
        head = slice(h * MLA_V, (h + 1) * MLA_V)
        o = (acc[0:MLA_V, :] / acc[MLA_V:MLA_V + 1, :]).T
        y_s[:, head] = (sg_ref[0, :, head].astype(_F32) * o).astype(_BF16)

    def finish():
        for h in heads:
            finish_head(h, acc_s[h])

    def fixed_reference_step(ki, last):
        for h, st in head_scores(ki):
            if last:
                st = jnp.where(visible, st, -jnp.inf)
            g_s[h] = jnp.maximum(g_s[h], jnp.max(st, axis=0, keepdims=True))
            pt = jnp.exp2(st - m_s[h]).astype(_BF16)
            acc = acc_s[h] + _dot(values(ki, h), pt)
            if last:
                finish_head(h, acc)
            else:
                acc_s[h] = acc

    def reset():
        m_s[...] = jnp.full_like(m_s, -jnp.inf)
        acc_s[...] = jnp.zeros_like(acc_s)

    @pl.when(qi == 0)
    def _():
        fixed_reference_first_step(last=True)

    @pl.when(qi > 0)
    def _():
        fixed_reference_first_step(last=False)
        lax.fori_loop(1, qi, lambda ki, c: (fixed_reference_step(ki, last=False), c)[1], 0)
        fixed_reference_step(qi, last=True)

    in_range = jnp.max(g_s[...] - m_s[...]) <= EXP2_RANGE

    @pl.when(jnp.logical_not(in_range))
    def _():
        reset()
        lax.fori_loop(0, qi + 1, running_max_step, 0)
        finish()

    y = _dot(y_s[...], wout_ref[...])
    o_ref[0] = h_ref[0] + y * _rms_scale(y) * gpost_ref[...]


def _attention_layer(h, qt, kn, kr, vt, sg, w_out, g_post):
    b, s, d = h.shape
    blk = ATTN_BLOCK
    assert vt.shape == (b, s // blk, MLA_WIDTH, blk)
    assert qt.shape == (b, s // blk, MLA_HEADS * MLA_QK_PAD, blk)
    whole = lambda i, j: (i, 0, 0)
    qblk = lambda i, j: (i, j, 0)
    const = lambda i, j: (0, 0)
    return pl.pallas_call(
        _attention_kernel,
        out_shape=jax.ShapeDtypeStruct((b, s, d), _F32),
        grid=(b, s // blk),
        in_specs=[
            pl.BlockSpec((1, 1) + qt.shape[2:], lambda i, j: (i, j, 0, 0)),
            pl.BlockSpec((1, s, kn.shape[-1]), whole),
            pl.BlockSpec((1, s, kr.shape[-1]), whole),
            pl.BlockSpec((1,) + vt.shape[1:], lambda i, j: (i, 0, 0, 0)),
            pl.BlockSpec((1, blk, MLA_WIDTH), qblk),
            pl.BlockSpec((1, blk, d), qblk),
            _single_buffered(w_out.shape, const),
            _single_buffered((1, d), const),
        ],
        out_specs=pl.BlockSpec((1, blk, d), qblk),
        scratch_shapes=[
            pltpu.VMEM((MLA_HEADS, 1, blk), _F32),
            pltpu.VMEM((MLA_HEADS, 1, blk), _F32),
            pltpu.VMEM((MLA_HEADS, MLA_V + ONES_ROWS, blk), _F32),
            pltpu.VMEM((blk, MLA_WIDTH), _BF16),
        ],
        compiler_params=pltpu.CompilerParams(
            dimension_semantics=("parallel", "arbitrary"),
            vmem_limit_bytes=VMEM_LIMIT_BYTES),
        name="latent_attention",
    )(qt, kn, kr, vt, sg, h, w_out.astype(_BF16), g_post.reshape(1, d))


@jax.jit
def kernel(x, g_pre_a, w_in_a, gn_gain_a, w_out_a, g_post_a, g_kv, w_kv_a, g_kv_lat, w_uk, w_uv,
           g_pre_b, w_in_b, g_q_lat, w_uq, w_out_b, g_post_b):
    assert w_in_a.shape[0] == 1 and w_in_b.shape[0] == 1
    h = _retention_layer(x, g_pre_a[0], w_in_a[0], gn_gain_a[0], w_out_a[0], g_post_a[0])
    qt, kn, kr, vt, sg = _projections(h, g_kv, w_kv_a, g_kv_lat, w_uk, w_uv,
                                      g_pre_b[0], w_in_b[0], g_q_lat[0], w_uq[0])
    return _attention_layer(h, qt, kn, kr, vt, sg, w_out_b[0], g_post_b[0])
```

```python
import jax
import jax.numpy as jnp
import numpy as np
from jax import lax
from jax.experimental import pallas as pl
from jax.experimental.pallas import tpu as pltpu

D_MODEL = 1024
CHUNK = 64
RMS_EPS = 1e-6
ROPE_BASE = 10000.0

RET_HEADS = 8
RET_QK = 128
RET_V = 256
RET_QK_WIDTH = RET_HEADS * RET_QK
RET_WIDTH = RET_HEADS * RET_V

MLA_HEADS = 8
MLA_NOPE = 128
MLA_ROPE = 64
MLA_V = 128
MLA_WIDTH = MLA_HEADS * MLA_V
Q_LORA = 384
KV_LORA = 256

LANES = 128
MLA_QK_PAD = 2 * LANES
VMEM_BYTES_V7X = 64 * 1024 * 1024
VMEM_LIMIT_BYTES = VMEM_BYTES_V7X - 4 * 1024 * 1024

RET_TOKEN_BLOCK = 512
RET_SPAN = 256
PROJ_TOKEN_BLOCK = 512
ATTN_BLOCK = PROJ_TOKEN_BLOCK
SCORE_LOOKAHEAD = 2
ONES_ROWS = 16
PROJECT_LAG = 2
EXP2_RANGE = 16.0

LOG2_E = 1.4426950408889634

_BF16 = jnp.bfloat16
_F32 = jnp.float32


def _single_buffered(shape, index_map):
    return pl.BlockSpec(shape, index_map, pipeline_mode=pl.Buffered(1))


def _rms_scale(x):
    return lax.rsqrt(jnp.mean(x * x, axis=-1, keepdims=True) + RMS_EPS)


def _rotate(x, cos, sin_signed):
    return x * cos + pltpu.roll(x, LANES // 2, axis=1) * sin_signed


def _dot(a, b):
    return jnp.dot(a, b, preferred_element_type=_F32)


def _dot_nt(a, b):
    return lax.dot_general(a, b, (((1,), (1,)), ((), ())), preferred_element_type=_F32)


def _dot_tn(a, b):
    return lax.dot_general(a, b, (((0,), (0,)), ((), ())), preferred_element_type=_F32)


def _retention_kernel(x_ref, gpre_ref, win_ref, cos_ref, sin_ref, dmask_ref, xi_ref, zeta_ref,
                      gspan_ref, gn_ref, wout_ref, gpost_ref, o_ref,
                      state_ref, q_s, k_s, v_s, gate_s, y_s):
    @pl.when(pl.program_id(1) == 0)
    def _():
        state_ref[...] = jnp.zeros_like(state_ref)

    x = x_ref[0]
    hb = (x * _rms_scale(x) * gpre_ref[...]).astype(_BF16)
    cos = cos_ref[...]
    sin = sin_ref[...]

    q_all = _dot(hb, win_ref[:, 0:RET_QK_WIDTH])
    for h in range(RET_HEADS):
        q_s[h] = _rotate(q_all[:, h * RET_QK:(h + 1) * RET_QK], cos, sin)
    k_all = _dot(hb, win_ref[:, RET_QK_WIDTH:2 * RET_QK_WIDTH])
    for h in range(RET_HEADS):
        k_s[h] = _rotate(k_all[:, h * RET_QK:(h + 1) * RET_QK], cos, sin)
    v_s[...] = _dot(hb, win_ref[:, 2 * RET_QK_WIDTH:2 * RET_QK_WIDTH + RET_WIDTH]).astype(_BF16)
    gate_s[...] = _dot(hb, win_ref[:, 2 * RET_QK_WIDTH + RET_WIDTH:])

    for s in range(RET_TOKEN_BLOCK // RET_SPAN):
        rows = slice(s * RET_SPAN, (s + 1) * RET_SPAN)
        for h in range(RET_HEADS):
            cols = slice(h * RET_V, (h + 1) * RET_V)
            q = q_s[h, rows, :]
            k = k_s[h, rows, :]
            v = v_s[rows, cols]
            scores = _dot_nt(q.astype(_BF16), k.astype(_BF16)) * dmask_ref[h]
            state = state_ref[h]
            o = _dot(scores.astype(_BF16), v)
            o = o + _dot((q * xi_ref[h]).astype(_BF16), state.astype(_BF16))
            state_ref[h] = state * gspan_ref[h] + _dot_tn((k * zeta_ref[h]).astype(_BF16), v)
            mu = jnp.mean(o, axis=-1, keepdims=True)
            oc = o - mu
            var = jnp.mean(oc * oc, axis=-1, keepdims=True)
            on = oc * lax.rsqrt(var + RMS_EPS) * gn_ref[:, cols]
            gate = gate_s[rows, cols]
            silu = gate / (1.0 + jnp.exp(-gate))
            y_s[rows, cols] = (silu * on).astype(_BF16)

    y = _dot(y_s[...], wout_ref[...])
    o_ref[0] = x + y * _rms_scale(y) * gpost_ref[...]


def _retention_tables():
    log_g = np.log(1.0 - np.exp2(-5.0 - np.arange(RET_HEADS, dtype=np.float64)))
    idx = np.arange(RET_SPAN, dtype=np.float64)
    diff = idx[:, None] - idx[None, :]
    chunk = np.arange(RET_SPAN) // CHUNK
    same = chunk[:, None] == chunk[None, :]
    earlier = chunk[None, :] < chunk[:, None]
    dist = np.where(same, np.abs(diff), diff)
    decay = np.exp(log_g[:, None, None] * np.where(same | earlier, dist, 0.0)[None])
    k_scale = RET_QK ** -0.5
    dmask = np.where((same | earlier)[None], decay, 0.0) * k_scale
    xi = np.exp(log_g[:, None] * (idx + 1.0)[None]) * k_scale
    zeta = np.exp(log_g[:, None] * (RET_SPAN - 1.0 - idx)[None])
    g_span = np.exp(log_g * RET_SPAN)
    xi = np.broadcast_to(xi[:, :, None], (RET_HEADS, RET_SPAN, RET_QK))
    zeta = np.broadcast_to(zeta[:, :, None], (RET_HEADS, RET_SPAN, RET_QK))
    g_span = np.broadcast_to(g_span[:, None, None], (RET_HEADS, 1, RET_V))
    return tuple(np.ascontiguousarray(t, dtype=np.float32) for t in (dmask, xi, zeta, g_span))


def _rope_angles(seq, half):
    inv = ROPE_BASE ** (-np.arange(half, dtype=np.float64) / half)
    ang = np.arange(seq, dtype=np.float64)[:, None] * inv[None, :]
    return np.cos(ang).astype(np.float32), np.sin(ang).astype(np.float32)


def _retention_layer(x, g_pre, w_in, gn_gain, w_out, g_post):
    b, s, d = x.shape
    tb = RET_TOKEN_BLOCK
    cos, sin = _rope_angles(s, RET_QK // 2)
    cos_t = np.concatenate([cos, cos], axis=-1)
    sin_t = np.concatenate([-sin, sin], axis=-1)
    dmask, xi, zeta, g_span = _retention_tables()
    const2 = lambda i, j: (0, 0)
    const3 = lambda i, j: (0, 0, 0)
    return pl.pallas_call(
        _retention_kernel,
        out_shape=jax.ShapeDtypeStruct((b, s, d), _F32),
        grid=(b, s // tb),
        in_specs=[
            pl.BlockSpec((1, tb, d), lambda i, j: (i, j, 0)),
            _single_buffered((1, d), const2),
            _single_buffered(w_in.shape, const2),
            pl.BlockSpec((tb, RET_QK), lambda i, j: (j, 0)),
            pl.BlockSpec((tb, RET_QK), lambda i, j: (j, 0)),
            _single_buffered(dmask.shape, const3),
            _single_buffered(xi.shape, const3),
            _single_buffered(zeta.shape, const3),
            _single_buffered(g_span.shape, const3),
            _single_buffered((1, RET_WIDTH), const2),
            _single_buffered(w_out.shape, const2),
            _single_buffered((1, d), const2),
        ],
        out_specs=pl.BlockSpec((1, tb, d), lambda i, j: (i, j, 0)),
        scratch_shapes=[
            pltpu.VMEM((RET_HEADS, RET_QK, RET_V), _F32),
            pltpu.VMEM((RET_HEADS, tb, RET_QK), _F32),
            pltpu.VMEM((RET_HEADS, tb, RET_QK), _F32),
            pltpu.VMEM((tb, RET_WIDTH), _BF16),
            pltpu.VMEM((tb, RET_WIDTH), _F32),
            pltpu.VMEM((tb, RET_WIDTH), _BF16),
        ],
        compiler_params=pltpu.CompilerParams(
            dimension_semantics=("parallel", "arbitrary"),
            vmem_limit_bytes=VMEM_LIMIT_BYTES),
        name="retention_layer",
    )(x, g_pre.reshape(1, d), w_in.astype(_BF16), cos_t, sin_t, dmask, xi, zeta, g_span,
      gn_gain.reshape(1, RET_WIDTH), w_out.astype(_BF16), g_post.reshape(1, d))


def _projection_kernel(h_ref, gkv_ref, wkva_ref, gkvlat_ref, wuk_ref, wuvt_ref, gpre_ref, winb_ref,
                       gqlat_ref, wuqt_ref, cos_ref, sin_ref, cost_ref, sint_ref,
                       qt_ref, kn_ref, kr_ref, vt_ref, sg_ref):
    x = h_ref[...]
    xn = x * _rms_scale(x)
    cos = cos_ref[...]
    sin = sin_ref[...]

    a = _dot((xn * gkv_ref[...]).astype(_BF16), wkva_ref[...])
    c_kv = a[:, :KV_LORA]
    kr_ref[...] = _rotate(a[:, KV_LORA:], cos, sin).astype(_BF16)
    c_kv = (c_kv * _rms_scale(c_kv) * gkvlat_ref[...]).astype(_BF16)
    kn_ref[...] = _dot(c_kv, wuk_ref[...]).astype(_BF16)
    vt_ref[0] = _dot_nt(wuvt_ref[...], c_kv).astype(_BF16)

    proj = _dot((xn * gpre_ref[...]).astype(_BF16), winb_ref[...])
    gate = proj[:, Q_LORA:]
    sg_ref[...] = (gate / (1.0 + jnp.exp(-gate))).astype(_BF16)
    c_q = proj[:, :Q_LORA]
    c_q = (c_q * _rms_scale(c_q) * gqlat_ref[...]).astype(_BF16)
    qt = _dot_nt(wuqt_ref[...], c_q)
    scale = (MLA_NOPE + MLA_ROPE) ** -0.5 * LOG2_E
    half = MLA_ROPE // 2
    cos_rows = cost_ref[...]
    sin_rows = sint_ref[...]
    zero_rows = jnp.zeros((LANES // 2 - half, qt.shape[1]), _BF16)
    for h in range(MLA_HEADS):
        src = h * (MLA_NOPE + MLA_ROPE)
        dst = h * MLA_QK_PAD
        qt_ref[0, dst:dst + MLA_NOPE, :] = (qt[src:src + MLA_NOPE] * scale).astype(_BF16)
        x1 = qt[src + MLA_NOPE:src + MLA_NOPE + half]
        x2 = qt[src + MLA_NOPE + half:src + MLA_NOPE + 2 * half]
        dst += MLA_NOPE
        qt_ref[0, dst:dst + half, :] = ((x1 * cos_rows - x2 * sin_rows) * scale).astype(_BF16)
        qt_ref[0, dst + half:dst + LANES // 2, :] = zero_rows
        dst += LANES // 2
        qt_ref[0, dst:dst + half, :] = ((x1 * sin_rows + x2 * cos_rows) * scale).astype(_BF16)
        qt_ref[0, dst + half:dst + LANES // 2, :] = zero_rows


def _spread_rope_columns(w):
    half = MLA_ROPE // 2
    z = jnp.zeros(w.shape[:-1] + (LANES // 2 - half,), w.dtype)
    return jnp.concatenate([w[..., :half], z, w[..., half:], z], axis=-1)


def _projections(h, g_kv, w_kv_a, g_kv_lat, w_uk, w_uv, g_pre, w_in, g_q_lat, w_uq):
    b, s, d = h.shape
    t = b * s
    tb = PROJ_TOKEN_BLOCK
    half = MLA_ROPE // 2
    cos, sin = _rope_angles(s, half)
    ones = np.ones((s, LANES // 2 - half), np.float32)
    zeros = np.zeros((s, LANES // 2 - half), np.float32)
    cos_t = np.concatenate([cos, ones, cos, ones], axis=-1)
    sin_t = np.concatenate([-sin, zeros, sin, zeros], axis=-1)

    wkva = jnp.concatenate([w_kv_a[:, :KV_LORA], _spread_rope_columns(w_kv_a[:, KV_LORA:])], axis=-1)

    const = lambda i: (0, 0)
    tok = lambda i: (i, 0)
    pos = lambda i: (i % (s // tb), 0)
    pos_t = lambda i: (0, i % (s // tb))
    weights = [
        g_kv.reshape(1, d), wkva.astype(_BF16), g_kv_lat.reshape(1, KV_LORA), w_uk.astype(_BF16),
        w_uv.T.astype(_BF16), g_pre.reshape(1, d), w_in.astype(_BF16), g_q_lat.reshape(1, Q_LORA),
        w_uq.T.astype(_BF16),
    ]

    def token_major(width):
        return jax.ShapeDtypeStruct((t, width), _BF16), pl.BlockSpec((tb, width), tok)

    def feature_major(width):
        return (jax.ShapeDtypeStruct((t // tb, width, tb), _BF16),
                pl.BlockSpec((1, width, tb), lambda i: (i, 0, 0)))

    outs = [feature_major(MLA_HEADS * MLA_QK_PAD), token_major(MLA_HEADS * MLA_NOPE),
            token_major(LANES), feature_major(MLA_WIDTH), token_major(MLA_WIDTH)]
    qt, kn, kr, vt, sg = pl.pallas_call(
        _projection_kernel,
        out_shape=[o[0] for o in outs],
        grid=(t // tb,),
        in_specs=[pl.BlockSpec((tb, d), tok)]
        + [_single_buffered(w.shape, const) for w in weights]
        + [pl.BlockSpec((tb, LANES), pos), pl.BlockSpec((tb, LANES), pos),
           pl.BlockSpec((half, tb), pos_t), pl.BlockSpec((half, tb), pos_t)],
        out_specs=[o[1] for o in outs],
        compiler_params=pltpu.CompilerParams(
            dimension_semantics=("parallel",),
            vmem_limit_bytes=VMEM_LIMIT_BYTES),
        name="latent_projections",
    )(h.reshape(t, d), *weights, cos_t, sin_t,
      np.ascontiguousarray(cos.T), np.ascontiguousarray(sin.T))
    kn, kr, sg = [o.reshape(b, s, -1) for o in (kn, kr, sg)]
    return (qt.reshape(b, s // tb, MLA_HEADS * MLA_QK_PAD, tb), kn, kr,
            vt.reshape(b, s // tb, MLA_WIDTH, tb), sg)


def _attention_kernel(qt_ref, kn_ref, kr_ref, vt_ref, sg_ref, h_ref, wout_ref, gpost_ref, o_ref,
                      m_s, g_s, acc_s, y_s):
    blk = ATTN_BLOCK
    qi = pl.program_id(1)
    key_chunk = lax.broadcasted_iota(jnp.int32, (blk, blk), 0) // CHUNK
    query_chunk = lax.broadcasted_iota(jnp.int32, (blk, blk), 1) // CHUNK
    visible = key_chunk <= query_chunk
    ones = jnp.ones((ONES_ROWS, blk), _BF16)

    heads = range(MLA_HEADS)

    def head_scores(ki):
        rows = pl.ds(pl.multiple_of(ki * blk, blk), blk)
        k_rope = kr_ref[0, rows, :]

        def score(h):
            qt = qt_ref[0, 0, h * MLA_QK_PAD:(h + 1) * MLA_QK_PAD, :]
            k = jnp.concatenate([kn_ref[0, rows, h * MLA_NOPE:(h + 1) * MLA_NOPE], k_rope], axis=-1)
            return _dot(k, qt)

        scores = [score(h) for h in heads[:SCORE_LOOKAHEAD]]
        for h in heads:
            if h + SCORE_LOOKAHEAD < MLA_HEADS:
                scores.append(score(h + SCORE_LOOKAHEAD))
            yield h, scores[h]

    def values(ki, h):
        return jnp.concatenate([vt_ref[0, ki, h * MLA_V:(h + 1) * MLA_V, :], ones], axis=0)

    def running_max_step(ki, carry=None):
        seen = jnp.logical_or(visible, ki < qi)
        for h, st in head_scores(ki):
            st = jnp.where(seen, st, -jnp.inf)
            m_prev = m_s[h]
            m_new = jnp.maximum(m_prev, jnp.max(st, axis=0, keepdims=True))
            pt = jnp.exp2(st - m_new).astype(_BF16)
            alpha = jnp.exp2(m_prev - m_new)
            acc_s[h] = alpha * acc_s[h] + _dot(values(ki, h), pt)
            m_s[h] = m_new
        return carry

    def fixed_reference_first_step(last):
        for h, st in head_scores(0):
            if last:
                st = jnp.where(visible, st, -jnp.inf)
            m_ref = jnp.max(st[0:CHUNK], axis=0, keepdims=True)
            m_s[h] = m_ref
            g_s[h] = jnp.max(st, axis=0, keepdims=True)
            pt = jnp.exp2(st - m_ref).astype(_BF16)
            acc = _dot(values(0, h), pt)
            if last:
                finish_head(h, acc)
                project_behind(h)
            else:
                acc_s[h] = acc

    def finish_head(h, acc):
        head = slice(h * MLA_V, (h + 1) * MLA_V)
        o = (acc[0:MLA_V, :] / acc[MLA_V:MLA_V + 1, :]).T
        y_s[:, head] = (sg_ref[0, :, head].astype(_F32) * o).astype(_BF16)

    def finish():
        for h in heads:
            finish_head(h, acc_s[h])

    def project(first_head, n_heads):
        cols = slice(first_head * MLA_V, (first_head + n_heads) * MLA_V)
        part = _dot(y_s[:, cols], wout_ref[cols, :])
        if first_head == 0:
            o_ref[0] = part
        else:
            o_ref[0] += part

    def project_behind(h):
        done = h - PROJECT_LAG
        if done >= 1 and done % 2 == 1:
            project(done - 1, 2)
        if h == MLA_HEADS - 1:
            for first in range(done + 1 - (done + 1) % 2, MLA_HEADS, 2):
                project(first, 2)

    def fixed_reference_step(ki, last):
        for h, st in head_scores(ki):
            if last:
                st = jnp.where(visible, st, -jnp.inf)
            g_s[h] = jnp.maximum(g_s[h], jnp.max(st, axis=0, keepdims=True))
            pt = jnp.exp2(st - m_s[h]).astype(_BF16)
            acc = acc_s[h] + _dot(values(ki, h), pt)
            if last:
                finish_head(h, acc)
                project_behind(h)
            else:
                acc_s[h] = acc

    def reset():
        m_s[...] = jnp.full_like(m_s, -jnp.inf)
        acc_s[...] = jnp.zeros_like(acc_s)

    @pl.when(qi == 0)
    def _():
        fixed_reference_first_step(last=True)

    @pl.when(qi > 0)
    def _():
        fixed_reference_first_step(last=False)
        lax.fori_loop(1, qi, lambda ki, c: (fixed_reference_step(ki, last=False), c)[1], 0)
        fixed_reference_step(qi, last=True)

    in_range = jnp.max(g_s[...] - m_s[...]) <= EXP2_RANGE

    @pl.when(jnp.logical_not(in_range))
    def _():
        reset()
        lax.fori_loop(0, qi + 1, running_max_step, 0)
        finish()
        project(0, MLA_HEADS)

    y = o_ref[0]
    o_ref[0] = h_ref[0] + y * _rms_scale(y) * gpost_ref[...]


def _attention_layer(h, qt, kn, kr, vt, sg, w_out, g_post):
    b, s, d = h.shape
    blk = ATTN_BLOCK
    assert vt.shape == (b, s // blk, MLA_WIDTH, blk)
    assert qt.shape == (b, s // blk, MLA_HEADS * MLA_QK_PAD, blk)
    whole = lambda i, j: (i, 0, 0)
    qblk = lambda i, j: (i, j, 0)
    const = lambda i, j: (0, 0)
    return pl.pallas_call(
        _attention_kernel,
        out_shape=jax.ShapeDtypeStruct((b, s, d), _F32),
        grid=(b, s // blk),
        in_specs=[
            pl.BlockSpec((1, 1) + qt.shape[2:], lambda i, j: (i, j, 0, 0)),
            pl.BlockSpec((1, s, kn.shape[-1]), whole),
            pl.BlockSpec((1, s, kr.shape[-1]), whole),
            pl.BlockSpec((1,) + vt.shape[1:], lambda i, j: (i, 0, 0, 0)),
            pl.BlockSpec((1, blk, MLA_WIDTH), qblk),
            pl.BlockSpec((1, blk, d), qblk),
            _single_buffered(w_out.shape, const),
            _single_buffered((1, d), const),
        ],
        out_specs=pl.BlockSpec((1, blk, d), qblk),
        scratch_shapes=[
            pltpu.VMEM((MLA_HEADS, 1, blk), _F32),
            pltpu.VMEM((MLA_HEADS, 1, blk), _F32),
            pltpu.VMEM((MLA_HEADS, MLA_V + ONES_ROWS, blk), _F32),
            pltpu.VMEM((blk, MLA_WIDTH), _BF16),
        ],
        compiler_params=pltpu.CompilerParams(
            dimension_semantics=("parallel", "arbitrary"),
            vmem_limit_bytes=VMEM_LIMIT_BYTES),
        name="latent_attention",
    )(qt, kn, kr, vt, sg, h, w_out.astype(_BF16), g_post.reshape(1, d))


@jax.jit
def kernel(x, g_pre_a, w_in_a, gn_gain_a, w_out_a, g_post_a, g_kv, w_kv_a, g_kv_lat, w_uk, w_uv,
           g_pre_b, w_in_b, g_q_lat, w_uq, w_out_b, g_post_b):
    assert w_in_a.shape[0] == 1 and w_in_b.shape[0] == 1
    h = _retention_layer(x, g_pre_a[0], w_in_a[0], gn_gain_a[0], w_out_a[0], g_post_a[0])
    qt, kn, kr, vt, sg = _projections(h, g_kv, w_kv_a, g_kv_lat, w_uk, w_uv,
                                      g_pre_b[0], w_in_b[0], g_q_lat[0], w_uq[0])
    return _attention_layer(h, qt, kn, kr, vt, sg, w_out_b[0], g_post_b[0])
```

```python
import jax
import jax.numpy as jnp
import numpy as np
from jax import lax
from jax.experimental import pallas as pl
from jax.experimental.pallas import tpu as pltpu

D_MODEL = 1024
CHUNK = 64
RMS_EPS = 1e-6
ROPE_BASE = 10000.0

RET_HEADS = 8
RET_QK = 128
RET_V = 256
RET_QK_WIDTH = RET_HEADS * RET_QK
RET_WIDTH = RET_HEADS * RET_V

MLA_HEADS = 8
MLA_NOPE = 128
MLA_ROPE = 64
MLA_V = 128
MLA_WIDTH = MLA_HEADS * MLA_V
Q_LORA = 384
KV_LORA = 256

LANES = 128
MLA_QK_PAD = 2 * LANES
VMEM_BYTES_V7X = 64 * 1024 * 1024
VMEM_LIMIT_BYTES = VMEM_BYTES_V7X - 4 * 1024 * 1024

RET_TOKEN_BLOCK = 512
RET_SPAN = 256
PROJ_TOKEN_BLOCK = 1024
ATTN_BLOCK = 512
SCORE_LOOKAHEAD = 2
ONES_ROWS = 16
EXP2_RANGE = 16.0

LOG2_E = 1.4426950408889634

_BF16 = jnp.bfloat16
_F32 = jnp.float32


def _single_buffered(shape, index_map):
    return pl.BlockSpec(shape, index_map, pipeline_mode=pl.Buffered(1))


def _rms_scale(x):
    return lax.rsqrt(jnp.mean(x * x, axis=-1, keepdims=True) + RMS_EPS)


def _rotate(x, cos, sin_signed):
    return x * cos + pltpu.roll(x, LANES // 2, axis=1) * sin_signed


def _dot(a, b):
    return jnp.dot(a, b, preferred_element_type=_F32)


def _dot_nt(a, b):
    return lax.dot_general(a, b, (((1,), (1,)), ((), ())), preferred_element_type=_F32)


def _dot_tn(a, b):
    return lax.dot_general(a, b, (((0,), (0,)), ((), ())), preferred_element_type=_F32)


def _retention_kernel(x_ref, gpre_ref, win_ref, cos_ref, sin_ref, dmask_ref, xi_ref, zeta_ref,
                      gspan_ref, gn_ref, wout_ref, gpost_ref, o_ref,
                      state_ref, q_s, k_s, v_s, gate_s, y_s):
    @pl.when(pl.program_id(1) == 0)
    def _():
        state_ref[...] = jnp.zeros_like(state_ref)

    x = x_ref[0]
    hb = (x * _rms_scale(x) * gpre_ref[...]).astype(_BF16)
    cos = cos_ref[...]
    sin = sin_ref[...]

    q_all = _dot(hb, win_ref[:, 0:RET_QK_WIDTH])
    for h in range(RET_HEADS):
        q_s[h] = _rotate(q_all[:, h * RET_QK:(h + 1) * RET_QK], cos, sin)
    k_all = _dot(hb, win_ref[:, RET_QK_WIDTH:2 * RET_QK_WIDTH])
    for h in range(RET_HEADS):
        k_s[h] = _rotate(k_all[:, h * RET_QK:(h + 1) * RET_QK], cos, sin)
    v_s[...] = _dot(hb, win_ref[:, 2 * RET_QK_WIDTH:2 * RET_QK_WIDTH + RET_WIDTH]).astype(_BF16)
    gate_s[...] = _dot(hb, win_ref[:, 2 * RET_QK_WIDTH + RET_WIDTH:])

    for s in range(RET_TOKEN_BLOCK // RET_SPAN):
        rows = slice(s * RET_SPAN, (s + 1) * RET_SPAN)
        for h in range(RET_HEADS):
            cols = slice(h * RET_V, (h + 1) * RET_V)
            q = q_s[h, rows, :]
            k = k_s[h, rows, :]
            v = v_s[rows, cols]
            scores = _dot_nt(q.astype(_BF16), k.astype(_BF16)) * dmask_ref[h]
            state = state_ref[h]
            o = _dot(scores.astype(_BF16), v)
            o = o + _dot((q * xi_ref[h]).astype(_BF16), state.astype(_BF16))
            state_ref[h] = state * gspan_ref[h] + _dot_tn((k * zeta_ref[h]).astype(_BF16), v)
            mu = jnp.mean(o, axis=-1, keepdims=True)
            oc = o - mu
            var = jnp.mean(oc * oc, axis=-1, keepdims=True)
            on = oc * lax.rsqrt(var + RMS_EPS) * gn_ref[:, cols]
            gate = gate_s[rows, cols]
            silu = gate / (1.0 + jnp.exp(-gate))
            y_s[rows, cols] = (silu * on).astype(_BF16)

    y = _dot(y_s[...], wout_ref[...])
    o_ref[0] = x + y * _rms_scale(y) * gpost_ref[...]


def _retention_tables():
    log_g = np.log(1.0 - np.exp2(-5.0 - np.arange(RET_HEADS, dtype=np.float64)))
    idx = np.arange(RET_SPAN, dtype=np.float64)
    diff = idx[:, None] - idx[None, :]
    chunk = np.arange(RET_SPAN) // CHUNK
    same = chunk[:, None] == chunk[None, :]
    earlier = chunk[None, :] < chunk[:, None]
    dist = np.where(same, np.abs(diff), diff)
    decay = np.exp(log_g[:, None, None] * np.where(same | earlier, dist, 0.0)[None])
    k_scale = RET_QK ** -0.5
    dmask = np.where((same | earlier)[None], decay, 0.0) * k_scale
    xi = np.exp(log_g[:, None] * (idx + 1.0)[None]) * k_scale
    zeta = np.exp(log_g[:, None] * (RET_SPAN - 1.0 - idx)[None])
    g_span = np.exp(log_g * RET_SPAN)
    xi = np.broadcast_to(xi[:, :, None], (RET_HEADS, RET_SPAN, RET_QK))
    zeta = np.broadcast_to(zeta[:, :, None], (RET_HEADS, RET_SPAN, RET_QK))
    g_span = np.broadcast_to(g_span[:, None, None], (RET_HEADS, 1, RET_V))
    return tuple(np.ascontiguousarray(t, dtype=np.float32) for t in (dmask, xi, zeta, g_span))


def _rope_angles(seq, half):
    inv = ROPE_BASE ** (-np.arange(half, dtype=np.float64) / half)
    ang = np.arange(seq, dtype=np.float64)[:, None] * inv[None, :]
    return np.cos(ang).astype(np.float32), np.sin(ang).astype(np.float32)


def _retention_layer(x, g_pre, w_in, gn_gain, w_out, g_post):
    b, s, d = x.shape
    tb = RET_TOKEN_BLOCK
    cos, sin = _rope_angles(s, RET_QK // 2)
    cos_t = np.concatenate([cos, cos], axis=-1)
    sin_t = np.concatenate([-sin, sin], axis=-1)
    dmask, xi, zeta, g_span = _retention_tables()
    const2 = lambda i, j: (0, 0)
    const3 = lambda i, j: (0, 0, 0)
    return pl.pallas_call(
        _retention_kernel,
        out_shape=jax.ShapeDtypeStruct((b, s, d), _F32),
        grid=(b, s // tb),
        in_specs=[
            pl.BlockSpec((1, tb, d), lambda i, j: (i, j, 0)),
            _single_buffered((1, d), const2),
            _single_buffered(w_in.shape, const2),
            pl.BlockSpec((tb, RET_QK), lambda i, j: (j, 0)),
            pl.BlockSpec((tb, RET_QK), lambda i, j: (j, 0)),
            _single_buffered(dmask.shape, const3),
            _single_buffered(xi.shape, const3),
            _single_buffered(zeta.shape, const3),
            _single_buffered(g_span.shape, const3),
            _single_buffered((1, RET_WIDTH), const2),
            _single_buffered(w_out.shape, const2),
            _single_buffered((1, d), const2),
        ],
        out_specs=pl.BlockSpec((1, tb, d), lambda i, j: (i, j, 0)),
        scratch_shapes=[
            pltpu.VMEM((RET_HEADS, RET_QK, RET_V), _F32),
            pltpu.VMEM((RET_HEADS, tb, RET_QK), _F32),
            pltpu.VMEM((RET_HEADS, tb, RET_QK), _F32),
            pltpu.VMEM((tb, RET_WIDTH), _BF16),
            pltpu.VMEM((tb, RET_WIDTH), _F32),
            pltpu.VMEM((tb, RET_WIDTH), _BF16),
        ],
        compiler_params=pltpu.CompilerParams(
            dimension_semantics=("parallel", "arbitrary"),
            vmem_limit_bytes=VMEM_LIMIT_BYTES),
        name="retention_layer",
    )(x, g_pre.reshape(1, d), w_in.astype(_BF16), cos_t, sin_t, dmask, xi, zeta, g_span,
      gn_gain.reshape(1, RET_WIDTH), w_out.astype(_BF16), g_post.reshape(1, d))


def _projection_kernel(h_ref, gkv_ref, wkva_ref, gkvlat_ref, wuk_ref, wuvt_ref, gpre_ref, winb_ref,
                       gqlat_ref, wuqt_ref, cos_ref, sin_ref, cost_ref, sint_ref,
                       qt_ref, kn_ref, kr_ref, vt_ref, sg_ref):
    def store_slabs(ref, rows, value):
        for i in range(value.shape[1] // ATTN_BLOCK):
            ref[i, rows, :] = value[:, i * ATTN_BLOCK:(i + 1) * ATTN_BLOCK].astype(_BF16)

    x = h_ref[...]
    xn = x * _rms_scale(x)
    cos = cos_ref[...]
    sin = sin_ref[...]

    a = _dot((xn * gkv_ref[...]).astype(_BF16), wkva_ref[...])
    c_kv = a[:, :KV_LORA]
    kr_ref[...] = _rotate(a[:, KV_LORA:], cos, sin).astype(_BF16)
    c_kv = (c_kv * _rms_scale(c_kv) * gkvlat_ref[...]).astype(_BF16)
    kn_ref[...] = _dot(c_kv, wuk_ref[...]).astype(_BF16)
    store_slabs(vt_ref, slice(None), _dot_nt(wuvt_ref[...], c_kv))

    proj = _dot((xn * gpre_ref[...]).astype(_BF16), winb_ref[...])
    gate = proj[:, Q_LORA:]
    sg_ref[...] = (gate / (1.0 + jnp.exp(-gate))).astype(_BF16)
    c_q = proj[:, :Q_LORA]
    c_q = (c_q * _rms_scale(c_q) * gqlat_ref[...]).astype(_BF16)
    qt = _dot_nt(wuqt_ref[...], c_q)
    scale = (MLA_NOPE + MLA_ROPE) ** -0.5 * LOG2_E
    half = MLA_ROPE // 2
    cos_rows = cost_ref[...]
    sin_rows = sint_ref[...]
    zero_rows = jnp.zeros((LANES // 2 - half, qt.shape[1]), _BF16)
    for h in range(MLA_HEADS):
        src = h * (MLA_NOPE + MLA_ROPE)
        dst = h * MLA_QK_PAD
        store_slabs(qt_ref, slice(dst, dst + MLA_NOPE), qt[src:src + MLA_NOPE] * scale)
        x1 = qt[src + MLA_NOPE:src + MLA_NOPE + half]
        x2 = qt[src + MLA_NOPE + half:src + MLA_NOPE + 2 * half]
        dst += MLA_NOPE
        store_slabs(qt_ref, slice(dst, dst + half), (x1 * cos_rows - x2 * sin_rows) * scale)
        store_slabs(qt_ref, slice(dst + half, dst + LANES // 2), zero_rows)
        dst += LANES // 2
        store_slabs(qt_ref, slice(dst, dst + half), (x1 * sin_rows + x2 * cos_rows) * scale)
        store_slabs(qt_ref, slice(dst + half, dst + LANES // 2), zero_rows)


def _spread_rope_columns(w):
    half = MLA_ROPE // 2
    z = jnp.zeros(w.shape[:-1] + (LANES // 2 - half,), w.dtype)
    return jnp.concatenate([w[..., :half], z, w[..., half:], z], axis=-1)


def _projections(h, g_kv, w_kv_a, g_kv_lat, w_uk, w_uv, g_pre, w_in, g_q_lat, w_uq):
    b, s, d = h.shape
    t = b * s
    tb = PROJ_TOKEN_BLOCK
    half = MLA_ROPE // 2
    cos, sin = _rope_angles(s, half)
    ones = np.ones((s, LANES // 2 - half), np.float32)
    zeros = np.zeros((s, LANES // 2 - half), np.float32)
    cos_t = np.concatenate([cos, ones, cos, ones], axis=-1)
    sin_t = np.concatenate([-sin, zeros, sin, zeros], axis=-1)

    wkva = jnp.concatenate([w_kv_a[:, :KV_LORA], _spread_rope_columns(w_kv_a[:, KV_LORA:])], axis=-1)

    const = lambda i: (0, 0)
    tok = lambda i: (i, 0)
    pos = lambda i: (i % (s // tb), 0)
    pos_t = lambda i: (0, i % (s // tb))
    weights = [
        g_kv.reshape(1, d), wkva.astype(_BF16), g_kv_lat.reshape(1, KV_LORA), w_uk.astype(_BF16),
        w_uv.T.astype(_BF16), g_pre.reshape(1, d), w_in.astype(_BF16), g_q_lat.reshape(1, Q_LORA),
        w_uq.T.astype(_BF16),
    ]

    def token_major(width):
        return jax.ShapeDtypeStruct((t, width), _BF16), pl.BlockSpec((tb, width), tok)

    def feature_major(width):
        return (jax.ShapeDtypeStruct((t // ATTN_BLOCK, width, ATTN_BLOCK), _BF16),
                pl.BlockSpec((tb // ATTN_BLOCK, width, ATTN_BLOCK), lambda i: (i, 0, 0)))

    outs = [feature_major(MLA_HEADS * MLA_QK_PAD), token_major(MLA_HEADS * MLA_NOPE),
            token_major(LANES), feature_major(MLA_WIDTH), token_major(MLA_WIDTH)]
    qt, kn, kr, vt, sg = pl.pallas_call(
        _projection_kernel,
        out_shape=[o[0] for o in outs],
        grid=(t // tb,),
        in_specs=[pl.BlockSpec((tb, d), tok)]
        + [_single_buffered(w.shape, const) for w in weights]
        + [pl.BlockSpec((tb, LANES), pos), pl.BlockSpec((tb, LANES), pos),
           pl.BlockSpec((half, tb), pos_t), pl.BlockSpec((half, tb), pos_t)],
        out_specs=[o[1] for o in outs],
        compiler_params=pltpu.CompilerParams(
            dimension_semantics=("parallel",),
            vmem_limit_bytes=VMEM_LIMIT_BYTES),
        name="latent_projections",
    )(h.reshape(t, d), *weights, cos_t, sin_t,
      np.ascontiguousarray(cos.T), np.ascontiguousarray(sin.T))
    kn, kr, sg = [o.reshape(b, s, -1) for o in (kn, kr, sg)]
    return (qt.reshape(b, s // ATTN_BLOCK, MLA_HEADS * MLA_QK_PAD, ATTN_BLOCK), kn, kr,
            vt.reshape(b, s // ATTN_BLOCK, MLA_WIDTH, ATTN_BLOCK), sg)


def _attention_kernel(qt_ref, kn_ref, kr_ref, vt_ref, sg_ref, h_ref, wout_ref, gpost_ref, o_ref,
                      m_s, g_s, acc_s, y_s):
    blk = ATTN_BLOCK
    qi = pl.program_id(1)
    key_chunk = lax.broadcasted_iota(jnp.int32, (blk, blk), 0) // CHUNK
    query_chunk = lax.broadcasted_iota(jnp.int32, (blk, blk), 1) // CHUNK
    visible = key_chunk <= query_chunk
    ones = jnp.ones((ONES_ROWS, blk), _BF16)

    heads = range(MLA_HEADS)

    def head_scores(ki):
        rows = pl.ds(pl.multiple_of(ki * blk, blk), blk)
        k_rope = kr_ref[0, rows, :]

        def score(h):
            qt = qt_ref[0, 0, h * MLA_QK_PAD:(h + 1) * MLA_QK_PAD, :]
            k = jnp.concatenate([kn_ref[0, rows, h * MLA_NOPE:(h + 1) * MLA_NOPE], k_rope], axis=-1)
            return _dot(k, qt)

        scores = [score(h) for h in heads[:SCORE_LOOKAHEAD]]
        for h in heads:
            if h + SCORE_LOOKAHEAD < MLA_HEADS:
                scores.append(score(h + SCORE_LOOKAHEAD))
            yield h, scores[h]

    def values(ki, h):
        return jnp.concatenate([vt_ref[0, ki, h * MLA_V:(h + 1) * MLA_V, :], ones], axis=0)

    def running_max_step(ki, carry=None):
        seen = jnp.logical_or(visible, ki < qi)
        for h, st in head_scores(ki):
            st = jnp.where(seen, st, -jnp.inf)
            m_prev = m_s[h]
            m_new = jnp.maximum(m_prev, jnp.max(st, axis=0, keepdims=True))
            pt = jnp.exp2(st - m_new).astype(_BF16)
            alpha = jnp.exp2(m_prev - m_new)
            acc_s[h] = alpha * acc_s[h] + _dot(values(ki, h), pt)
            m_s[h] = m_new
        return carry

    def fixed_reference_first_step(last):
        for h, st in head_scores(0):
            if last:
                st = jnp.where(visible, st, -jnp.inf)
            m_ref = jnp.max(st[0:CHUNK], axis=0, keepdims=True)
            m_s[h] = m_ref
            g_s[h] = jnp.max(st, axis=0, keepdims=True)
            pt = jnp.exp2(st - m_ref).astype(_BF16)
            acc = _dot(values(0, h), pt)
            if last:
                finish_head(h, acc)
            else:
                acc_s[h] = acc

    def finish_head(h, acc):
        head = slice(h * MLA_V, (h + 1) * MLA_V)
        o = (acc[0:MLA_V, :] / acc[MLA_V:MLA_V + 1, :]).T
        y_s[:, head] = (sg_ref[0, :, head].astype(_F32) * o).astype(_BF16)

    def finish():
        for h in heads:
            finish_head(h, acc_s[h])

    def fixed_reference_step(ki, last):
        for h, st in head_scores(ki):
            if last:
                st = jnp.where(visible, st, -jnp.inf)
            g_s[h] = jnp.maximum(g_s[h], jnp.max(st, axis=0, keepdims=True))
            pt = jnp.exp2(st - m_s[h]).astype(_BF16)
            acc = acc_s[h] + _dot(values(ki, h), pt)
            if last:
                finish_head(h, acc)
            else:
                acc_s[h] = acc

    def reset():
        m_s[...] = jnp.full_like(m_s, -jnp.inf)
        acc_s[...] = jnp.zeros_like(acc_s)

    @pl.when(qi == 0)
    def _():
        fixed_reference_first_step(last=True)

    @pl.when(qi > 0)
    def _():
        fixed_reference_first_step(last=False)
        lax.fori_loop(1, qi, lambda ki, c: (fixed_reference_step(ki, last=False), c)[1], 0)
        fixed_reference_step(qi, last=True)

    in_range = jnp.max(g_s[...] - m_s[...]) <= EXP2_RANGE

    @pl.when(jnp.logical_not(in_range))
    def _():
        reset()
        lax.fori_loop(0, qi + 1, running_max_step, 0)
        finish()

    y = _dot(y_s[...], wout_ref[...])
    o_ref[0] = h_ref[0] + y * _rms_scale(y) * gpost_ref[...]


def _attention_layer(h, qt, kn, kr, vt, sg, w_out, g_post):
    b, s, d = h.shape
    blk = ATTN_BLOCK
    assert vt.shape == (b, s // blk, MLA_WIDTH, blk)
    assert qt.shape == (b, s // blk, MLA_HEADS * MLA_QK_PAD, blk)
    whole = lambda i, j: (i, 0, 0)
    qblk = lambda i, j: (i, j, 0)
    const = lambda i, j: (0, 0)
    return pl.pallas_call(
        _attention_kernel,
        out_shape=jax.ShapeDtypeStruct((b, s, d), _F32),
        grid=(b, s // blk),
        in_specs=[
            pl.BlockSpec((1, 1) + qt.shape[2:], lambda i, j: (i, j, 0, 0)),
            pl.BlockSpec((1, s, kn.shape[-1]), whole),
            pl.BlockSpec((1, s, kr.shape[-1]), whole),
            pl.BlockSpec((1,) + vt.shape[1:], lambda i, j: (i, 0, 0, 0)),
            pl.BlockSpec((1, blk, MLA_WIDTH), qblk),
            pl.BlockSpec((1, blk, d), qblk),
            _single_buffered(w_out.shape, const),
            _single_buffered((1, d), const),
        ],
        out_specs=pl.BlockSpec((1, blk, d), qblk),
        scratch_shapes=[
            pltpu.VMEM((MLA_HEADS, 1, blk), _F32),
            pltpu.VMEM((MLA_HEADS, 1, blk), _F32),
            pltpu.VMEM((MLA_HEADS, MLA_V + ONES_ROWS, blk), _F32),
            pltpu.VMEM((blk, MLA_WIDTH), _BF16),
        ],
        compiler_params=pltpu.CompilerParams(
            dimension_semantics=("parallel", "arbitrary"),
            vmem_limit_bytes=VMEM_LIMIT_BYTES),
        name="latent_attention",
    )(qt, kn, kr, vt, sg, h, w_out.astype(_BF16), g_post.reshape(1, d))


@jax.jit
def kernel(x, g_pre_a, w_in_a, gn_gain_a, w_out_a, g_post_a, g_kv, w_kv_a, g_kv_lat, w_uk, w_uv,
           g_pre_b, w_in_b, g_q_lat, w_uq, w_out_b, g_post_b):
    assert w_in_a.shape[0] == 1 and w_in_b.shape[0] == 1
    h = _retention_layer(x, g_pre_a[0], w_in_a[0], gn_gain_a[0], w_out_a[0], g_post_a[0])
    qt, kn, kr, vt, sg = _projections(h, g_kv, w_kv_a, g_kv_lat, w_uk, w_uv,
                                      g_pre_b[0], w_in_b[0], g_q_lat[0], w_uq[0])
    return _attention_layer(h, qt, kn, kr, vt, sg, w_out_b[0], g_post_b[0])
```

```python
import jax
import jax.numpy as jnp
import numpy as np
from jax import lax
from jax.experimental import pallas as pl
from jax.experimental.pallas import tpu as pltpu

D_MODEL = 1024
CHUNK = 64
RMS_EPS = 1e-6
ROPE_BASE = 10000.0

RET_HEADS = 8
RET_QK = 128
RET_V = 256
RET_QK_WIDTH = RET_HEADS * RET_QK
RET_WIDTH = RET_HEADS * RET_V

MLA_HEADS = 8
MLA_NOPE = 128
MLA_ROPE = 64
MLA_V = 128
MLA_WIDTH = MLA_HEADS * MLA_V
Q_LORA = 384
KV_LORA = 256

LANES = 128
MLA_QK_PAD = 2 * LANES
VMEM_BYTES_V7X = 64 * 1024 * 1024
VMEM_LIMIT_BYTES = VMEM_BYTES_V7X - 4 * 1024 * 1024

RET_TOKEN_BLOCK = 512
RET_SPAN = 256
PROJ_TOKEN_BLOCK = 1024
ATTN_BLOCK = 512
SCORE_LOOKAHEAD = 2
ONES_ROWS = 16
MAX_DENOMINATOR = 2.0 ** 24

LOG2_E = 1.4426950408889634

_BF16 = jnp.bfloat16
_F32 = jnp.float32


def _single_buffered(shape, index_map):
    return pl.BlockSpec(shape, index_map, pipeline_mode=pl.Buffered(1))


def _rms_scale(x):
    return lax.rsqrt(jnp.mean(x * x, axis=-1, keepdims=True) + RMS_EPS)


def _rotate(x, cos, sin_signed):
    return x * cos + pltpu.roll(x, LANES // 2, axis=1) * sin_signed


def _dot(a, b):
    return jnp.dot(a, b, preferred_element_type=_F32)


def _dot_nt(a, b):
    return lax.dot_general(a, b, (((1,), (1,)), ((), ())), preferred_element_type=_F32)


def _dot_tn(a, b):
    return lax.dot_general(a, b, (((0,), (0,)), ((), ())), preferred_element_type=_F32)


def _retention_kernel(x_ref, gpre_ref, win_ref, cos_ref, sin_ref, dmask_ref, xi_ref, zeta_ref,
                      gspan_ref, gn_ref, wout_ref, gpost_ref, o_ref,
                      state_ref, q_s, k_s, v_s, gate_s, y_s):
    @pl.when(pl.program_id(1) == 0)
    def _():
        state_ref[...] = jnp.zeros_like(state_ref)

    x = x_ref[0]
    hb = (x * _rms_scale(x) * gpre_ref[...]).astype(_BF16)
    cos = cos_ref[...]
    sin = sin_ref[...]

    q_all = _dot(hb, win_ref[:, 0:RET_QK_WIDTH])
    for h in range(RET_HEADS):
        q_s[h] = _rotate(q_all[:, h * RET_QK:(h + 1) * RET_QK], cos, sin)
    k_all = _dot(hb, win_ref[:, RET_QK_WIDTH:2 * RET_QK_WIDTH])
    for h in range(RET_HEADS):
        k_s[h] = _rotate(k_all[:, h * RET_QK:(h + 1) * RET_QK], cos, sin)
    v_s[...] = _dot(hb, win_ref[:, 2 * RET_QK_WIDTH:2 * RET_QK_WIDTH + RET_WIDTH]).astype(_BF16)
    gate_s[...] = _dot(hb, win_ref[:, 2 * RET_QK_WIDTH + RET_WIDTH:])

    for s in range(RET_TOKEN_BLOCK // RET_SPAN):
        rows = slice(s * RET_SPAN, (s + 1) * RET_SPAN)
        for h in range(RET_HEADS):
            cols = slice(h * RET_V, (h + 1) * RET_V)
            q = q_s[h, rows, :]
            k = k_s[h, rows, :]
            v = v_s[rows, cols]
            scores = _dot_nt(q.astype(_BF16), k.astype(_BF16)) * dmask_ref[h]
            state = state_ref[h]
            o = _dot(scores.astype(_BF16), v)
            o = o + _dot((q * xi_ref[h]).astype(_BF16), state.astype(_BF16))
            state_ref[h] = state * gspan_ref[h] + _dot_tn((k * zeta_ref[h]).astype(_BF16), v)
            mu = jnp.mean(o, axis=-1, keepdims=True)
            oc = o - mu
            var = jnp.mean(oc * oc, axis=-1, keepdims=True)
            on = oc * lax.rsqrt(var + RMS_EPS) * gn_ref[:, cols]
            gate = gate_s[rows, cols]
            silu = gate / (1.0 + jnp.exp(-gate))
            y_s[rows, cols] = (silu * on).astype(_BF16)

    y = _dot(y_s[...], wout_ref[...])
    o_ref[0] = x + y * _rms_scale(y) * gpost_ref[...]


def _retention_tables():
    log_g = np.log(1.0 - np.exp2(-5.0 - np.arange(RET_HEADS, dtype=np.float64)))
    idx = np.arange(RET_SPAN, dtype=np.float64)
    diff = idx[:, None] - idx[None, :]
    chunk = np.arange(RET_SPAN) // CHUNK
    same = chunk[:, None] == chunk[None, :]
    earlier = chunk[None, :] < chunk[:, None]
    dist = np.where(same, np.abs(diff), diff)
    decay = np.exp(log_g[:, None, None] * np.where(same | earlier, dist, 0.0)[None])
    k_scale = RET_QK ** -0.5
    dmask = np.where((same | earlier)[None], decay, 0.0) * k_scale
    xi = np.exp(log_g[:, None] * (idx + 1.0)[None]) * k_scale
    zeta = np.exp(log_g[:, None] * (RET_SPAN - 1.0 - idx)[None])
    g_span = np.exp(log_g * RET_SPAN)
    xi = np.broadcast_to(xi[:, :, None], (RET_HEADS, RET_SPAN, RET_QK))
    zeta = np.broadcast_to(zeta[:, :, None], (RET_HEADS, RET_SPAN, RET_QK))
    g_span = np.broadcast_to(g_span[:, None, None], (RET_HEADS, 1, RET_V))
    return tuple(np.ascontiguousarray(t, dtype=np.float32) for t in (dmask, xi, zeta, g_span))


def _rope_angles(seq, half):
    inv = ROPE_BASE ** (-np.arange(half, dtype=np.float64) / half)
    ang = np.arange(seq, dtype=np.float64)[:, None] * inv[None, :]
    return np.cos(ang).astype(np.float32), np.sin(ang).astype(np.float32)


def _retention_layer(x, g_pre, w_in, gn_gain, w_out, g_post):
    b, s, d = x.shape
    tb = RET_TOKEN_BLOCK
    cos, sin = _rope_angles(s, RET_QK // 2)
    cos_t = np.concatenate([cos, cos], axis=-1)
    sin_t = np.concatenate([-sin, sin], axis=-1)
    dmask, xi, zeta, g_span = _retention_tables()
    const2 = lambda i, j: (0, 0)
    const3 = lambda i, j: (0, 0, 0)
    return pl.pallas_call(
        _retention_kernel,
        out_shape=jax.ShapeDtypeStruct((b, s, d), _F32),
        grid=(b, s // tb),
        in_specs=[
            pl.BlockSpec((1, tb, d), lambda i, j: (i, j, 0)),
            _single_buffered((1, d), const2),
            _single_buffered(w_in.shape, const2),
            pl.BlockSpec((tb, RET_QK), lambda i, j: (j, 0)),
            pl.BlockSpec((tb, RET_QK), lambda i, j: (j, 0)),
            _single_buffered(dmask.shape, const3),
            _single_buffered(xi.shape, const3),
            _single_buffered(zeta.shape, const3),
            _single_buffered(g_span.shape, const3),
            _single_buffered((1, RET_WIDTH), const2),
            _single_buffered(w_out.shape, const2),
            _single_buffered((1, d), const2),
        ],
        out_specs=pl.BlockSpec((1, tb, d), lambda i, j: (i, j, 0)),
        scratch_shapes=[
            pltpu.VMEM((RET_HEADS, RET_QK, RET_V), _F32),
            pltpu.VMEM((RET_HEADS, tb, RET_QK), _F32),
            pltpu.VMEM((RET_HEADS, tb, RET_QK), _F32),
            pltpu.VMEM((tb, RET_WIDTH), _BF16),
            pltpu.VMEM((tb, RET_WIDTH), _F32),
            pltpu.VMEM((tb, RET_WIDTH), _BF16),
        ],
        compiler_params=pltpu.CompilerParams(
            dimension_semantics=("parallel", "arbitrary"),
            vmem_limit_bytes=VMEM_LIMIT_BYTES),
        name="retention_layer",
    )(x, g_pre.reshape(1, d), w_in.astype(_BF16), cos_t, sin_t, dmask, xi, zeta, g_span,
      gn_gain.reshape(1, RET_WIDTH), w_out.astype(_BF16), g_post.reshape(1, d))


def _projection_kernel(h_ref, gkv_ref, wkva_ref, gkvlat_ref, wuk_ref, wuvt_ref, gpre_ref, winb_ref,
                       gqlat_ref, wuqt_ref, cos_ref, sin_ref, cost_ref, sint_ref,
                       qt_ref, kn_ref, kr_ref, vt_ref, sg_ref):
    def store_slabs(ref, rows, value):
        for i in range(value.shape[1] // ATTN_BLOCK):
            ref[i, rows, :] = value[:, i * ATTN_BLOCK:(i + 1) * ATTN_BLOCK].astype(_BF16)

    x = h_ref[...]
    xn = x * _rms_scale(x)
    cos = cos_ref[...]
    sin = sin_ref[...]

    a = _dot((xn * gkv_ref[...]).astype(_BF16), wkva_ref[...])
    c_kv = a[:, :KV_LORA]
    kr_ref[...] = _rotate(a[:, KV_LORA:], cos, sin).astype(_BF16)
    c_kv = (c_kv * _rms_scale(c_kv) * gkvlat_ref[...]).astype(_BF16)
    kn_ref[...] = _dot(c_kv, wuk_ref[...]).astype(_BF16)
    store_slabs(vt_ref, slice(None), _dot_nt(wuvt_ref[...], c_kv))

    proj = _dot((xn * gpre_ref[...]).astype(_BF16), winb_ref[...])
    gate = proj[:, Q_LORA:]
    sg_ref[...] = (gate / (1.0 + jnp.exp(-gate))).astype(_BF16)
    c_q = proj[:, :Q_LORA]
    c_q = (c_q * _rms_scale(c_q) * gqlat_ref[...]).astype(_BF16)
    qt = _dot_nt(wuqt_ref[...], c_q)
    scale = (MLA_NOPE + MLA_ROPE) ** -0.5 * LOG2_E
    half = MLA_ROPE // 2
    cos_rows = cost_ref[...]
    sin_rows = sint_ref[...]
    zero_rows = jnp.zeros((LANES // 2 - half, qt.shape[1]), _BF16)
    for h in range(MLA_HEADS):
        src = h * (MLA_NOPE + MLA_ROPE)
        dst = h * MLA_QK_PAD
        store_slabs(qt_ref, slice(dst, dst + MLA_NOPE), qt[src:src + MLA_NOPE] * scale)
        x1 = qt[src + MLA_NOPE:src + MLA_NOPE + half]
        x2 = qt[src + MLA_NOPE + half:src + MLA_NOPE + 2 * half]
        dst += MLA_NOPE
        store_slabs(qt_ref, slice(dst, dst + half), (x1 * cos_rows - x2 * sin_rows) * scale)
        store_slabs(qt_ref, slice(dst + half, dst + LANES // 2), zero_rows)
        dst += LANES // 2
        store_slabs(qt_ref, slice(dst, dst + half), (x1 * sin_rows + x2 * cos_rows) * scale)
        store_slabs(qt_ref, slice(dst + half, dst + LANES // 2), zero_rows)


def _spread_rope_columns(w):
    half = MLA_ROPE // 2
    z = jnp.zeros(w.shape[:-1] + (LANES // 2 - half,), w.dtype)
    return jnp.concatenate([w[..., :half], z, w[..., half:], z], axis=-1)


def _projections(h, g_kv, w_kv_a, g_kv_lat, w_uk, w_uv, g_pre, w_in, g_q_lat, w_uq):
    b, s, d = h.shape
    t = b * s
    tb = PROJ_TOKEN_BLOCK
    half = MLA_ROPE // 2
    cos, sin = _rope_angles(s, half)
    ones = np.ones((s, LANES // 2 - half), np.float32)
    zeros = np.zeros((s, LANES // 2 - half), np.float32)
    cos_t = np.concatenate([cos, ones, cos, ones], axis=-1)
    sin_t = np.concatenate([-sin, zeros, sin, zeros], axis=-1)

    wkva = jnp.concatenate([w_kv_a[:, :KV_LORA], _spread_rope_columns(w_kv_a[:, KV_LORA:])], axis=-1)

    const = lambda i: (0, 0)
    tok = lambda i: (i, 0)
    pos = lambda i: (i % (s // tb), 0)
    pos_t = lambda i: (0, i % (s // tb))
    weights = [
        g_kv.reshape(1, d), wkva.astype(_BF16), g_kv_lat.reshape(1, KV_LORA), w_uk.astype(_BF16),
        w_uv.T.astype(_BF16), g_pre.reshape(1, d), w_in.astype(_BF16), g_q_lat.reshape(1, Q_LORA),
        w_uq.T.astype(_BF16),
    ]

    def token_major(width):
        return jax.ShapeDtypeStruct((t, width), _BF16), pl.BlockSpec((tb, width), tok)

    def feature_major(width):
        return (jax.ShapeDtypeStruct((t // ATTN_BLOCK, width, ATTN_BLOCK), _BF16),
                pl.BlockSpec((tb // ATTN_BLOCK, width, ATTN_BLOCK), lambda i: (i, 0, 0)))

    outs = [feature_major(MLA_HEADS * MLA_QK_PAD), token_major(MLA_HEADS * MLA_NOPE),
            token_major(LANES), feature_major(MLA_WIDTH), token_major(MLA_WIDTH)]
    qt, kn, kr, vt, sg = pl.pallas_call(
        _projection_kernel,
        out_shape=[o[0] for o in outs],
        grid=(t // tb,),
        in_specs=[pl.BlockSpec((tb, d), tok)]
        + [_single_buffered(w.shape, const) for w in weights]
        + [pl.BlockSpec((tb, LANES), pos), pl.BlockSpec((tb, LANES), pos),
           pl.BlockSpec((half, tb), pos_t), pl.BlockSpec((half, tb), pos_t)],
        out_specs=[o[1] for o in outs],
        compiler_params=pltpu.CompilerParams(
            dimension_semantics=("parallel",),
            vmem_limit_bytes=VMEM_LIMIT_BYTES),
        name="latent_projections",
    )(h.reshape(t, d), *weights, cos_t, sin_t,
      np.ascontiguousarray(cos.T), np.ascontiguousarray(sin.T))
    kn, kr, sg = [o.reshape(b, s, -1) for o in (kn, kr, sg)]
    return (qt.reshape(b, s // ATTN_BLOCK, MLA_HEADS * MLA_QK_PAD, ATTN_BLOCK), kn, kr,
            vt.reshape(b, s // ATTN_BLOCK, MLA_WIDTH, ATTN_BLOCK), sg)


def _attention_kernel(qt_ref, kn_ref, kr_ref, vt_ref, sg_ref, h_ref, wout_ref, gpost_ref, o_ref,
                      m_s, den_s, acc_s, y_s):
    blk = ATTN_BLOCK
    qi = pl.program_id(1)
    key_chunk = lax.broadcasted_iota(jnp.int32, (blk, blk), 0) // CHUNK
    query_chunk = lax.broadcasted_iota(jnp.int32, (blk, blk), 1) // CHUNK
    visible = key_chunk <= query_chunk
    ones = jnp.ones((ONES_ROWS, blk), _BF16)

    heads = range(MLA_HEADS)

    def head_scores(ki):
        rows = pl.ds(pl.multiple_of(ki * blk, blk), blk)
        k_rope = kr_ref[0, rows, :]

        def score(h):
            qt = qt_ref[0, 0, h * MLA_QK_PAD:(h + 1) * MLA_QK_PAD, :]
            k = jnp.concatenate([kn_ref[0, rows, h * MLA_NOPE:(h + 1) * MLA_NOPE], k_rope], axis=-1)
            return _dot(k, qt)

        scores = [score(h) for h in heads[:SCORE_LOOKAHEAD]]
        for h in heads:
            if h + SCORE_LOOKAHEAD < MLA_HEADS:
                scores.append(score(h + SCORE_LOOKAHEAD))
            yield h, scores[h]

    def values(ki, h):
        return jnp.concatenate([vt_ref[0, ki, h * MLA_V:(h + 1) * MLA_V, :], ones], axis=0)

    def running_max_step(ki, carry=None):
        seen = jnp.logical_or(visible, ki < qi)
        for h, st in head_scores(ki):
            st = jnp.where(seen, st, -jnp.inf)
            m_prev = m_s[h]
            m_new = jnp.maximum(m_prev, jnp.max(st, axis=0, keepdims=True))
            pt = jnp.exp2(st - m_new).astype(_BF16)
            alpha = jnp.exp2(m_prev - m_new)
            acc_s[h] = alpha * acc_s[h] + _dot(values(ki, h), pt)
            m_s[h] = m_new
        return carry

    def fixed_reference_first_step(last):
        for h, st in head_scores(0):
            if last:
                st = jnp.where(visible, st, -jnp.inf)
            m_ref = jnp.max(st[0:CHUNK], axis=0, keepdims=True)
            m_s[h] = m_ref
            pt = jnp.exp2(st - m_ref).astype(_BF16)
            acc = _dot(values(0, h), pt)
            if last:
                finish_head(h, acc)
            else:
                acc_s[h] = acc

    def finish_head(h, acc):
        head = slice(h * MLA_V, (h + 1) * MLA_V)
        den_s[h] = acc[MLA_V:MLA_V + 1, :]
        o = (acc[0:MLA_V, :] / acc[MLA_V:MLA_V + 1, :]).T
        y_s[:, head] = (sg_ref[0, :, head].astype(_F32) * o).astype(_BF16)

    def finish():
        for h in heads:
            finish_head(h, acc_s[h])

    def fixed_reference_step(ki, last):
        for h, st in head_scores(ki):
            if last:
                st = jnp.where(visible, st, -jnp.inf)
            pt = jnp.exp2(st - m_s[h]).astype(_BF16)
            acc = acc_s[h] + _dot(values(ki, h), pt)
            if last:
                finish_head(h, acc)
            else:
                acc_s[h] = acc

    def reset():
        m_s[...] = jnp.full_like(m_s, -jnp.inf)
        acc_s[...] = jnp.zeros_like(acc_s)

    @pl.when(qi == 0)
    def _():
        fixed_reference_first_step(last=True)

    @pl.when(qi > 0)
    def _():
        fixed_reference_first_step(last=False)
        lax.fori_loop(1, qi, lambda ki, c: (fixed_reference_step(ki, last=False), c)[1], 0)
        fixed_reference_step(qi, last=True)

    in_range = jnp.max(den_s[...]) <= MAX_DENOMINATOR

    @pl.when(jnp.logical_not(in_range))
    def _():
        reset()
        lax.fori_loop(0, qi + 1, running_max_step, 0)
        finish()

    y = _dot(y_s[...], wout_ref[...])
    o_ref[0] = h_ref[0] + y * _rms_scale(y) * gpost_ref[...]


def _attention_layer(h, qt, kn, kr, vt, sg, w_out, g_post):
    b, s, d = h.shape
    blk = ATTN_BLOCK
    assert vt.shape == (b, s // blk, MLA_WIDTH, blk)
    assert qt.shape == (b, s // blk, MLA_HEADS * MLA_QK_PAD, blk)
    whole = lambda i, j: (i, 0, 0)
    qblk = lambda i, j: (i, j, 0)
    const = lambda i, j: (0, 0)
    return pl.pallas_call(
        _attention_kernel,
        out_shape=jax.ShapeDtypeStruct((b, s, d), _F32),
        grid=(b, s // blk),
        in_specs=[
            pl.BlockSpec((1, 1) + qt.shape[2:], lambda i, j: (i, j, 0, 0)),
            pl.BlockSpec((1, s, kn.shape[-1]), whole),
            pl.BlockSpec((1, s, kr.shape[-1]), whole),
            pl.BlockSpec((1,) + vt.shape[1:], lambda i, j: (i, 0, 0, 0)),
            pl.BlockSpec((1, blk, MLA_WIDTH), qblk),
            pl.BlockSpec((1, blk, d), qblk),
            _single_buffered(w_out.shape, const),
            _single_buffered((1, d), const),
        ],
        out_specs=pl.BlockSpec((1, blk, d), qblk),
        scratch_shapes=[
            pltpu.VMEM((MLA_HEADS, 1, blk), _F32),
            pltpu.VMEM((MLA_HEADS, 1, blk), _F32),
            pltpu.VMEM((MLA_HEADS, MLA_V + ONES_ROWS, blk), _F32),
            pltpu.VMEM((blk, MLA_WIDTH), _BF16),
        ],
        compiler_params=pltpu.CompilerParams(
            dimension_semantics=("parallel", "arbitrary"),
            vmem_limit_bytes=VMEM_LIMIT_BYTES),
        name="latent_attention",
    )(qt, kn, kr, vt, sg, h, w_out.astype(_BF16), g_post.reshape(1, d))


@jax.jit
def kernel(x, g_pre_a, w_in_a, gn_gain_a, w_out_a, g_post_a, g_kv, w_kv_a, g_kv_lat, w_uk, w_uv,
           g_pre_b, w_in_b, g_q_lat, w_uq, w_out_b, g_post_b):
    assert w_in_a.shape[0] == 1 and w_in_b.shape[0] == 1
    h = _retention_layer(x, g_pre_a[0], w_in_a[0], gn_gain_a[0], w_out_a[0], g_post_a[0])
    qt, kn, kr, vt, sg = _projections(h, g_kv, w_kv_a, g_kv_lat, w_uk, w_uv,
                                      g_pre_b[0], w_in_b[0], g_q_lat[0], w_uq[0])
    return _attention_layer(h, qt, kn, kr, vt, sg, w_out_b[0], g_post_b[0])
```

```python
import jax
import jax.numpy as jnp
import numpy as np
from jax import lax
from jax.experimental import pallas as pl
from jax.experimental.pallas import tpu as pltpu

D_MODEL = 1024
CHUNK = 64
RMS_EPS = 1e-6
ROPE_BASE = 10000.0

RET_HEADS = 8
RET_QK = 128
RET_V = 256
RET_QK_WIDTH = RET_HEADS * RET_QK
RET_WIDTH = RET_HEADS * RET_V

MLA_HEADS = 8
MLA_NOPE = 128
MLA_ROPE = 64
MLA_V = 128
MLA_WIDTH = MLA_HEADS * MLA_V
Q_LORA = 384
KV_LORA = 256

LANES = 128
MLA_QK_PAD = 2 * LANES
VMEM_BYTES_V7X = 64 * 1024 * 1024
VMEM_LIMIT_BYTES = VMEM_BYTES_V7X - 4 * 1024 * 1024

RET_TOKEN_BLOCK = 512
RET_SPAN = 256
PROJ_TOKEN_BLOCK = 1024
ATTN_BLOCK = 512
SCORE_LOOKAHEAD = 2
ONES_ROWS = 16
MAX_DENOMINATOR = 2.0 ** 24

LOG2_E = 1.4426950408889634

_BF16 = jnp.bfloat16
_F32 = jnp.float32


def _single_buffered(shape, index_map):
    return pl.BlockSpec(shape, index_map, pipeline_mode=pl.Buffered(1))


def _rms_scale(x):
    return lax.rsqrt(jnp.mean(x * x, axis=-1, keepdims=True) + RMS_EPS)


def _rotate(x, cos, sin_signed):
    return x * cos + pltpu.roll(x, LANES // 2, axis=1) * sin_signed


def _dot(a, b):
    return jnp.dot(a, b, preferred_element_type=_F32)


def _dot_nt(a, b):
    return lax.dot_general(a, b, (((1,), (1,)), ((), ())), preferred_element_type=_F32)


def _dot_tn(a, b):
    return lax.dot_general(a, b, (((0,), (0,)), ((), ())), preferred_element_type=_F32)


def _retention_kernel(x_ref, gpre_ref, win_ref, cos_ref, sin_ref, dmask_ref, xi_ref, zeta_ref,
                      gspan_ref, gn_ref, wout_ref, gpost_ref, o_ref,
                      state_ref, q_s, k_s, v_s, gate_s, y_s):
    @pl.when(pl.program_id(1) == 0)
    def _():
        state_ref[...] = jnp.zeros_like(state_ref)

    x = x_ref[0]
    hb = (x * _rms_scale(x) * gpre_ref[...]).astype(_BF16)
    cos = cos_ref[...]
    sin = sin_ref[...]

    q_all = _dot(hb, win_ref[:, 0:RET_QK_WIDTH])
    for h in range(RET_HEADS):
        q_s[h] = _rotate(q_all[:, h * RET_QK:(h + 1) * RET_QK], cos, sin)
    k_all = _dot(hb, win_ref[:, RET_QK_WIDTH:2 * RET_QK_WIDTH])
    for h in range(RET_HEADS):
        k_s[h] = _rotate(k_all[:, h * RET_QK:(h + 1) * RET_QK], cos, sin)
    v_s[...] = _dot(hb, win_ref[:, 2 * RET_QK_WIDTH:2 * RET_QK_WIDTH + RET_WIDTH]).astype(_BF16)
    gate_s[...] = _dot(hb, win_ref[:, 2 * RET_QK_WIDTH + RET_WIDTH:])

    for s in range(RET_TOKEN_BLOCK // RET_SPAN):
        rows = slice(s * RET_SPAN, (s + 1) * RET_SPAN)
        for h in range(RET_HEADS):
            cols = slice(h * RET_V, (h + 1) * RET_V)
            q = q_s[h, rows, :]
            k = k_s[h, rows, :]
            v = v_s[rows, cols]
            scores = _dot_nt(q.astype(_BF16), k.astype(_BF16)) * dmask_ref[h]
            state = state_ref[h]
            o = _dot(scores.astype(_BF16), v)
            o = o + _dot((q * xi_ref[h]).astype(_BF16), state.astype(_BF16))
            state_ref[h] = state * gspan_ref[h] + _dot_tn((k * zeta_ref[h]).astype(_BF16), v)
            mu = jnp.mean(o, axis=-1, keepdims=True)
            oc = o - mu
            var = jnp.mean(oc * oc, axis=-1, keepdims=True)
            on = oc * lax.rsqrt(var + RMS_EPS) * gn_ref[:, cols]
            half_gate = 0.5 * gate_s[rows, cols]
            silu = half_gate * jnp.tanh(half_gate) + half_gate
            y_s[rows, cols] = (silu * on).astype(_BF16)

    y = _dot(y_s[...], wout_ref[...])
    o_ref[0] = x + y * _rms_scale(y) * gpost_ref[...]


def _retention_tables():
    log_g = np.log(1.0 - np.exp2(-5.0 - np.arange(RET_HEADS, dtype=np.float64)))
    idx = np.arange(RET_SPAN, dtype=np.float64)
    diff = idx[:, None] - idx[None, :]
    chunk = np.arange(RET_SPAN) // CHUNK
    same = chunk[:, None] == chunk[None, :]
    earlier = chunk[None, :] < chunk[:, None]
    dist = np.where(same, np.abs(diff), diff)
    decay = np.exp(log_g[:, None, None] * np.where(same | earlier, dist, 0.0)[None])
    k_scale = RET_QK ** -0.5
    dmask = np.where((same | earlier)[None], decay, 0.0) * k_scale
    xi = np.exp(log_g[:, None] * (idx + 1.0)[None]) * k_scale
    zeta = np.exp(log_g[:, None] * (RET_SPAN - 1.0 - idx)[None])
    g_span = np.exp(log_g * RET_SPAN)
    xi = np.broadcast_to(xi[:, :, None], (RET_HEADS, RET_SPAN, RET_QK))
    zeta = np.broadcast_to(zeta[:, :, None], (RET_HEADS, RET_SPAN, RET_QK))
    g_span = np.broadcast_to(g_span[:, None, None], (RET_HEADS, 1, RET_V))
    return tuple(np.ascontiguousarray(t, dtype=np.float32) for t in (dmask, xi, zeta, g_span))


def _rope_angles(seq, half):
    inv = ROPE_BASE ** (-np.arange(half, dtype=np.float64) / half)
    ang = np.arange(seq, dtype=np.float64)[:, None] * inv[None, :]
    return np.cos(ang).astype(np.float32), np.sin(ang).astype(np.float32)


def _retention_layer(x, g_pre, w_in, gn_gain, w_out, g_post):
    b, s, d = x.shape
    tb = RET_TOKEN_BLOCK
    cos, sin = _rope_angles(s, RET_QK // 2)
    cos_t = np.concatenate([cos, cos], axis=-1)
    sin_t = np.concatenate([-sin, sin], axis=-1)
    dmask, xi, zeta, g_span = _retention_tables()
    const2 = lambda i, j: (0, 0)
    const3 = lambda i, j: (0, 0, 0)
    return pl.pallas_call(
        _retention_kernel,
        out_shape=jax.ShapeDtypeStruct((b, s, d), _F32),
        grid=(b, s // tb),
        in_specs=[
            pl.BlockSpec((1, tb, d), lambda i, j: (i, j, 0)),
            _single_buffered((1, d), const2),
            _single_buffered(w_in.shape, const2),
            pl.BlockSpec((tb, RET_QK), lambda i, j: (j, 0)),
            pl.BlockSpec((tb, RET_QK), lambda i, j: (j, 0)),
            _single_buffered(dmask.shape, const3),
            _single_buffered(xi.shape, const3),
            _single_buffered(zeta.shape, const3),
            _single_buffered(g_span.shape, const3),
            _single_buffered((1, RET_WIDTH), const2),
            _single_buffered(w_out.shape, const2),
            _single_buffered((1, d), const2),
        ],
        out_specs=pl.BlockSpec((1, tb, d), lambda i, j: (i, j, 0)),
        scratch_shapes=[
            pltpu.VMEM((RET_HEADS, RET_QK, RET_V), _F32),
            pltpu.VMEM((RET_HEADS, tb, RET_QK), _F32),
            pltpu.VMEM((RET_HEADS, tb, RET_QK), _F32),
            pltpu.VMEM((tb, RET_WIDTH), _BF16),
            pltpu.VMEM((tb, RET_WIDTH), _F32),
            pltpu.VMEM((tb, RET_WIDTH), _BF16),
        ],
        compiler_params=pltpu.CompilerParams(
            dimension_semantics=("parallel", "arbitrary"),
            vmem_limit_bytes=VMEM_LIMIT_BYTES),
        name="retention_layer",
    )(x, g_pre.reshape(1, d), w_in.astype(_BF16), cos_t, sin_t, dmask, xi, zeta, g_span,
      gn_gain.reshape(1, RET_WIDTH), w_out.astype(_BF16), g_post.reshape(1, d))


def _projection_kernel(h_ref, gkv_ref, wkva_ref, gkvlat_ref, wuk_ref, wuvt_ref, gpre_ref, winb_ref,
                       gqlat_ref, wuqt_ref, cos_ref, sin_ref, cost_ref, sint_ref,
                       qt_ref, kn_ref, kr_ref, vt_ref, sg_ref):
    def store_slabs(ref, rows, value):
        for i in range(value.shape[1] // ATTN_BLOCK):
            ref[i, rows, :] = value[:, i * ATTN_BLOCK:(i + 1) * ATTN_BLOCK].astype(_BF16)

    x = h_ref[...]
    xn = x * _rms_scale(x)
    cos = cos_ref[...]
    sin = sin_ref[...]

    a = _dot((xn * gkv_ref[...]).astype(_BF16), wkva_ref[...])
    c_kv = a[:, :KV_LORA]
    kr_ref[...] = _rotate(a[:, KV_LORA:], cos, sin).astype(_BF16)
    c_kv = (c_kv * _rms_scale(c_kv) * gkvlat_ref[...]).astype(_BF16)
    kn_ref[...] = _dot(c_kv, wuk_ref[...]).astype(_BF16)
    store_slabs(vt_ref, slice(None), _dot_nt(wuvt_ref[...], c_kv))

    proj = _dot((xn * gpre_ref[...]).astype(_BF16), winb_ref[...])
    half_gate = 0.5 * proj[:, Q_LORA:]
    sg_ref[...] = (half_gate * jnp.tanh(half_gate) + half_gate).astype(_BF16)
    c_q = proj[:, :Q_LORA]
    c_q = (c_q * _rms_scale(c_q) * gqlat_ref[...]).astype(_BF16)
    qt = _dot_nt(wuqt_ref[...], c_q)
    scale = (MLA_NOPE + MLA_ROPE) ** -0.5 * LOG2_E
    half = MLA_ROPE // 2
    cos_rows = cost_ref[...]
    sin_rows = sint_ref[...]
    zero_rows = jnp.zeros((LANES // 2 - half, qt.shape[1]), _BF16)
    for h in range(MLA_HEADS):
        src = h * (MLA_NOPE + MLA_ROPE)
        dst = h * MLA_QK_PAD
        store_slabs(qt_ref, slice(dst, dst + MLA_NOPE), qt[src:src + MLA_NOPE] * scale)
        x1 = qt[src + MLA_NOPE:src + MLA_NOPE + half]
        x2 = qt[src + MLA_NOPE + half:src + MLA_NOPE + 2 * half]
        dst += MLA_NOPE
        store_slabs(qt_ref, slice(dst, dst + half), (x1 * cos_rows - x2 * sin_rows) * scale)
        store_slabs(qt_ref, slice(dst + half, dst + LANES // 2), zero_rows)
        dst += LANES // 2
        store_slabs(qt_ref, slice(dst, dst + half), (x1 * sin_rows + x2 * cos_rows) * scale)
        store_slabs(qt_ref, slice(dst + half, dst + LANES // 2), zero_rows)


def _spread_rope_columns(w):
    half = MLA_ROPE // 2
    z = jnp.zeros(w.shape[:-1] + (LANES // 2 - half,), w.dtype)
    return jnp.concatenate([w[..., :half], z, w[..., half:], z], axis=-1)


def _projections(h, g_kv, w_kv_a, g_kv_lat, w_uk, w_uv, g_pre, w_in, g_q_lat, w_uq):
    b, s, d = h.shape
    t = b * s
    tb = PROJ_TOKEN_BLOCK
    half = MLA_ROPE // 2
    cos, sin = _rope_angles(s, half)
    ones = np.ones((s, LANES // 2 - half), np.float32)
    zeros = np.zeros((s, LANES // 2 - half), np.float32)
    cos_t = np.concatenate([cos, ones, cos, ones], axis=-1)
    sin_t = np.concatenate([-sin, zeros, sin, zeros], axis=-1)

    wkva = jnp.concatenate([w_kv_a[:, :KV_LORA], _spread_rope_columns(w_kv_a[:, KV_LORA:])], axis=-1)

    const = lambda i: (0, 0)
    tok = lambda i: (i, 0)
    pos = lambda i: (i % (s // tb), 0)
    pos_t = lambda i: (0, i % (s // tb))
    weights = [
        g_kv.reshape(1, d), wkva.astype(_BF16), g_kv_lat.reshape(1, KV_LORA), w_uk.astype(_BF16),
        w_uv.T.astype(_BF16), g_pre.reshape(1, d), w_in.astype(_BF16), g_q_lat.reshape(1, Q_LORA),
        w_uq.T.astype(_BF16),
    ]

    def token_major(width):
        return jax.ShapeDtypeStruct((t, width), _BF16), pl.BlockSpec((tb, width), tok)

    def feature_major(width):
        return (jax.ShapeDtypeStruct((t // ATTN_BLOCK, width, ATTN_BLOCK), _BF16),
                pl.BlockSpec((tb // ATTN_BLOCK, width, ATTN_BLOCK), lambda i: (i, 0, 0)))

    outs = [feature_major(MLA_HEADS * MLA_QK_PAD), token_major(MLA_HEADS * MLA_NOPE),
            token_major(LANES), feature_major(MLA_WIDTH), token_major(MLA_WIDTH)]
    qt, kn, kr, vt, sg = pl.pallas_call(
        _projection_kernel,
        out_shape=[o[0] for o in outs],
        grid=(t // tb,),
        in_specs=[pl.BlockSpec((tb, d), tok)]
        + [_single_buffered(w.shape, const) for w in weights]
        + [pl.BlockSpec((tb, LANES), pos), pl.BlockSpec((tb, LANES), pos),
           pl.BlockSpec((half, tb), pos_t), pl.BlockSpec((half, tb), pos_t)],
        out_specs=[o[1] for o in outs],
        compiler_params=pltpu.CompilerParams(
            dimension_semantics=("parallel",),
            vmem_limit_bytes=VMEM_LIMIT_BYTES),
        name="latent_projections",
    )(h.reshape(t, d), *weights, cos_t, sin_t,
      np.ascontiguousarray(cos.T), np.ascontiguousarray(sin.T))
    kn, kr, sg = [o.reshape(b, s, -1) for o in (kn, kr, sg)]
    return (qt.reshape(b, s // ATTN_BLOCK, MLA_HEADS * MLA_QK_PAD, ATTN_BLOCK), kn, kr,
            vt.reshape(b, s // ATTN_BLOCK, MLA_WIDTH, ATTN_BLOCK), sg)


def _attention_kernel(qt_ref, kn_ref, kr_ref, vt_ref, sg_ref, h_ref, wout_ref, gpost_ref, o_ref,
                      m_s, den_s, acc_s, y_s):
    blk = ATTN_BLOCK
    qi = pl.program_id(1)
    key_chunk = lax.broadcasted_iota(jnp.int32, (blk, blk), 0) // CHUNK
    query_chunk = lax.broadcasted_iota(jnp.int32, (blk, blk), 1) // CHUNK
    visible = key_chunk <= query_chunk
    ones = jnp.ones((ONES_ROWS, blk), _BF16)

    heads = range(MLA_HEADS)

    def head_scores(ki):
        rows = pl.ds(pl.multiple_of(ki * blk, blk), blk)
        k_rope = kr_ref[0, rows, :]

        def score(h):
            qt = qt_ref[0, 0, h * MLA_QK_PAD:(h + 1) * MLA_QK_PAD, :]
            k = jnp.concatenate([kn_ref[0, rows, h * MLA_NOPE:(h + 1) * MLA_NOPE], k_rope], axis=-1)
            return _dot(k, qt)

        scores = [score(h) for h in heads[:SCORE_LOOKAHEAD]]
        for h in heads:
            if h + SCORE_LOOKAHEAD < MLA_HEADS:
                scores.append(score(h + SCORE_LOOKAHEAD))
            yield h, scores[h]

    def values(ki, h):
        return jnp.concatenate([vt_ref[0, ki, h * MLA_V:(h + 1) * MLA_V, :], ones], axis=0)

    def running_max_step(ki, carry=None):
        seen = jnp.logical_or(visible, ki < qi)
        for h, st in head_scores(ki):
            st = jnp.where(seen, st, -jnp.inf)
            m_prev = m_s[h]
            m_new = jnp.maximum(m_prev, jnp.max(st, axis=0, keepdims=True))
            pt = jnp.exp2(st - m_new).astype(_BF16)
            alpha = jnp.exp2(m_prev - m_new)
            acc_s[h] = alpha * acc_s[h] + _dot(values(ki, h), pt)
            m_s[h] = m_new
        return carry

    def fixed_reference_first_step(last):
        for h, st in head_scores(0):
            if last:
                st = jnp.where(visible, st, -jnp.inf)
            m_ref = jnp.max(st[0:CHUNK], axis=0, keepdims=True)
            m_s[h] = m_ref
            pt = jnp.exp2(st - m_ref).astype(_BF16)
            acc = _dot(values(0, h), pt)
            if last:
                finish_head(h, acc)
            else:
                acc_s[h] = acc

    def finish_head(h, acc):
        head = slice(h * MLA_V, (h + 1) * MLA_V)
        den_s[h] = acc[MLA_V:MLA_V + 1, :]
        o = (acc[0:MLA_V, :] / acc[MLA_V:MLA_V + 1, :]).T
        y_s[:, head] = (sg_ref[0, :, head].astype(_F32) * o).astype(_BF16)

    def finish():
        for h in heads:
            finish_head(h, acc_s[h])

    def fixed_reference_step(ki, last):
        for h, st in head_scores(ki):
            if last:
                st = jnp.where(visible, st, -jnp.inf)
            pt = jnp.exp2(st - m_s[h]).astype(_BF16)
            acc = acc_s[h] + _dot(values(ki, h), pt)
            if last:
                finish_head(h, acc)
            else:
                acc_s[h] = acc

    def reset():
        m_s[...] = jnp.full_like(m_s, -jnp.inf)
        acc_s[...] = jnp.zeros_like(acc_s)

    @pl.when(qi == 0)
    def _():
        fixed_reference_first_step(last=True)

    @pl.when(qi > 0)
    def _():
        fixed_reference_first_step(last=False)
        lax.fori_loop(1, qi, lambda ki, c: (fixed_reference_step(ki, last=False), c)[1], 0)
        fixed_reference_step(qi, last=True)

    in_range = jnp.max(den_s[...]) <= MAX_DENOMINATOR

    @pl.when(jnp.logical_not(in_range))
    def _():
        reset()
        lax.fori_loop(0, qi + 1, running_max_step, 0)
        finish()

    y = _dot(y_s[...], wout_ref[...])
    o_ref[0] = h_ref[0] + y * _rms_scale(y) * gpost_ref[...]


def _attention_layer(h, qt, kn, kr, vt, sg, w_out, g_post):
    b, s, d = h.shape
    blk = ATTN_BLOCK
    assert vt.shape == (b, s // blk, MLA_WIDTH, blk)
    assert qt.shape == (b, s // blk, MLA_HEADS * MLA_QK_PAD, blk)
    whole = lambda i, j: (i, 0, 0)
    qblk = lambda i, j: (i, j, 0)
    const = lambda i, j: (0, 0)
    return pl.pallas_call(
        _attention_kernel,
        out_shape=jax.ShapeDtypeStruct((b, s, d), _F32),
        grid=(b, s // blk),
        in_specs=[
            pl.BlockSpec((1, 1) + qt.shape[2:], lambda i, j: (i, j, 0, 0)),
            pl.BlockSpec((1, s, kn.shape[-1]), whole),
            pl.BlockSpec((1, s, kr.shape[-1]), whole),
            pl.BlockSpec((1,) + vt.shape[1:], lambda i, j: (i, 0, 0, 0)),
            pl.BlockSpec((1, blk, MLA_WIDTH), qblk),
            pl.BlockSpec((1, blk, d), qblk),
            _single_buffered(w_out.shape, const),
            _single_buffered((1, d), const),
        ],
        out_specs=pl.BlockSpec((1, blk, d), qblk),
        scratch_shapes=[
            pltpu.VMEM((MLA_HEADS, 1, blk), _F32),
            pltpu.VMEM((MLA_HEADS, 1, blk), _F32),
            pltpu.VMEM((MLA_HEADS, MLA_V + ONES_ROWS, blk), _F32),
            pltpu.VMEM((blk, MLA_WIDTH), _BF16),
        ],
        compiler_params=pltpu.CompilerParams(
            dimension_semantics=("parallel", "arbitrary"),
            vmem_limit_bytes=VMEM_LIMIT_BYTES),
        name="latent_attention",
    )(qt, kn, kr, vt, sg, h, w_out.astype(_BF16), g_post.reshape(1, d))


@jax.jit
def kernel(x, g_pre_a, w_in_a, gn_gain_a, w_out_a, g_post_a, g_kv, w_kv_a, g_kv_lat, w_uk, w_uv,
           g_pre_b, w_in_b, g_q_lat, w_uq, w_out_b, g_post_b):
    assert w_in_a.shape[0] == 1 and w_in_b.shape[0] == 1
    h = _retention_layer(x, g_pre_a[0], w_in_a[0], gn_gain_a[0], w_out_a[0], g_post_a[0])
    qt, kn, kr, vt, sg = _projections(h, g_kv, w_kv_a, g_kv_lat, w_uk, w_uv,
                                      g_pre_b[0], w_in_b[0], g_q_lat[0], w_uq[0])
    return _attention_layer(h, qt, kn, kr, vt, sg, w_out_b[0], g_post_b[0])
```

```python
import jax
import jax.numpy as jnp
import numpy as np
from jax import lax
from jax.experimental import pallas as pl
from jax.experimental.pallas import tpu as pltpu

D_MODEL = 1024
CHUNK = 64
RMS_EPS = 1e-6
ROPE_BASE = 10000.0

RET_HEADS = 8
RET_QK = 128
RET_V = 256
RET_QK_WIDTH = RET_HEADS * RET_QK
RET_WIDTH = RET_HEADS * RET_V

MLA_HEADS = 8
MLA_NOPE = 128
MLA_ROPE = 64
MLA_V = 128
MLA_WIDTH = MLA_HEADS * MLA_V
Q_LORA = 384
KV_LORA = 256

LANES = 128
MLA_QK_PAD = 2 * LANES
VMEM_BYTES_V7X = 64 * 1024 * 1024
VMEM_LIMIT_BYTES = VMEM_BYTES_V7X - 4 * 1024 * 1024

RET_TOKEN_BLOCK = 512
RET_SPAN = 256
PROJ_TOKEN_BLOCK = 1024
ATTN_BLOCK = 512
SCORE_LOOKAHEAD = 2
ONES_ROWS = 16
MAX_DENOMINATOR = 2.0 ** 24

LOG2_E = 1.4426950408889634

_BF16 = jnp.bfloat16
_F32 = jnp.float32


def _single_buffered(shape, index_map):
    return pl.BlockSpec(shape, index_map, pipeline_mode=pl.Buffered(1))


def _rms_scale(x):
    return lax.rsqrt(jnp.mean(x * x, axis=-1, keepdims=True) + RMS_EPS)


def _rotate(x, cos, sin_signed):
    return x * cos + pltpu.roll(x, LANES // 2, axis=1) * sin_signed


def _dot(a, b):
    return jnp.dot(a, b, preferred_element_type=_F32)


def _dot_nt(a, b):
    return lax.dot_general(a, b, (((1,), (1,)), ((), ())), preferred_element_type=_F32)


def _dot_tn(a, b):
    return lax.dot_general(a, b, (((0,), (0,)), ((), ())), preferred_element_type=_F32)


def _retention_kernel(x_ref, gpre_ref, win_ref, cos_ref, sin_ref, dmask_ref, xi_ref, zeta_ref,
                      gspan_ref, gn_ref, wout_ref, gpost_ref, o_ref,
                      state_ref, q_s, k_s, v_s, gate_s, y_s):
    @pl.when(pl.program_id(1) == 0)
    def _():
        state_ref[...] = jnp.zeros_like(state_ref)

    x = x_ref[0]
    hb = (x * _rms_scale(x) * gpre_ref[...]).astype(_BF16)
    cos = cos_ref[...]
    sin = sin_ref[...]

    q_all = _dot(hb, win_ref[:, 0:RET_QK_WIDTH])
    for h in range(RET_HEADS):
        q_s[h] = _rotate(q_all[:, h * RET_QK:(h + 1) * RET_QK], cos, sin)
    k_all = _dot(hb, win_ref[:, RET_QK_WIDTH:2 * RET_QK_WIDTH])
    for h in range(RET_HEADS):
        k_s[h] = _rotate(k_all[:, h * RET_QK:(h + 1) * RET_QK], cos, sin)
    v_s[...] = _dot(hb, win_ref[:, 2 * RET_QK_WIDTH:2 * RET_QK_WIDTH + RET_WIDTH]).astype(_BF16)
    gate_s[...] = _dot(hb, win_ref[:, 2 * RET_QK_WIDTH + RET_WIDTH:])

    for s in range(RET_TOKEN_BLOCK // RET_SPAN):
        rows = slice(s * RET_SPAN, (s + 1) * RET_SPAN)
        for h in range(RET_HEADS):
            cols = slice(h * RET_V, (h + 1) * RET_V)
            q = q_s[h, rows, :]
            k = k_s[h, rows, :]
            v = v_s[rows, cols]
            scores = _dot_nt(q.astype(_BF16), k.astype(_BF16)) * dmask_ref[h]
            state = state_ref[h]
            o = _dot(scores.astype(_BF16), v)
            o = o + _dot((q * xi_ref[h]).astype(_BF16), state.astype(_BF16))
            state_ref[h] = state * gspan_ref[h] + _dot_tn((k * zeta_ref[h]).astype(_BF16), v)
            mu = jnp.mean(o, axis=-1, keepdims=True)
            oc = o - mu
            var = jnp.mean(oc * oc, axis=-1, keepdims=True)
            on = oc * lax.rsqrt(var + RMS_EPS) * gn_ref[:, cols]
            half_gate = 0.5 * gate_s[rows, cols]
            silu = half_gate * jnp.tanh(half_gate) + half_gate
            y_s[rows, cols] = (silu * on).astype(_BF16)

    y = _dot(y_s[...], wout_ref[...])
    o_ref[0] = x + y * _rms_scale(y) * gpost_ref[...]


def _retention_tables():
    log_g = np.log(1.0 - np.exp2(-5.0 - np.arange(RET_HEADS, dtype=np.float64)))
    idx = np.arange(RET_SPAN, dtype=np.float64)
    diff = idx[:, None] - idx[None, :]
    chunk = np.arange(RET_SPAN) // CHUNK
    same = chunk[:, None] == chunk[None, :]
    earlier = chunk[None, :] < chunk[:, None]
    dist = np.where(same, np.abs(diff), diff)
    decay = np.exp(log_g[:, None, None] * np.where(same | earlier, dist, 0.0)[None])
    k_scale = RET_QK ** -0.5
    dmask = np.where((same | earlier)[None], decay, 0.0) * k_scale
    xi = np.exp(log_g[:, None] * (idx + 1.0)[None]) * k_scale
    zeta = np.exp(log_g[:, None] * (RET_SPAN - 1.0 - idx)[None])
    g_span = np.exp(log_g * RET_SPAN)
    xi = np.broadcast_to(xi[:, :, None], (RET_HEADS, RET_SPAN, RET_QK))
    zeta = np.broadcast_to(zeta[:, :, None], (RET_HEADS, RET_SPAN, RET_QK))
    g_span = np.broadcast_to(g_span[:, None, None], (RET_HEADS, 1, RET_V))
    return tuple(np.ascontiguousarray(t, dtype=np.float32) for t in (dmask, xi, zeta, g_span))


def _rope_angles(seq, half):
    inv = ROPE_BASE ** (-np.arange(half, dtype=np.float64) / half)
    ang = np.arange(seq, dtype=np.float64)[:, None] * inv[None, :]
    return np.cos(ang).astype(np.float32), np.sin(ang).astype(np.float32)


def _retention_layer(x, g_pre, w_in, gn_gain, w_out, g_post):
    b, s, d = x.shape
    tb = RET_TOKEN_BLOCK
    cos, sin = _rope_angles(s, RET_QK // 2)
    cos_t = np.concatenate([cos, cos], axis=-1)
    sin_t = np.concatenate([-sin, sin], axis=-1)
    dmask, xi, zeta, g_span = _retention_tables()
    const2 = lambda i, j: (0, 0)
    const3 = lambda i, j: (0, 0, 0)
    return pl.pallas_call(
        _retention_kernel,
        out_shape=jax.ShapeDtypeStruct((b, s, d), _F32),
        grid=(b, s // tb),
        in_specs=[
            pl.BlockSpec((1, tb, d), lambda i, j: (i, j, 0)),
            _single_buffered((1, d), const2),
            _single_buffered(w_in.shape, const2),
            pl.BlockSpec((tb, RET_QK), lambda i, j: (j, 0)),
            pl.BlockSpec((tb, RET_QK), lambda i, j: (j, 0)),
            _single_buffered(dmask.shape, const3),
            _single_buffered(xi.shape, const3),
            _single_buffered(zeta.shape, const3),
            _single_buffered(g_span.shape, const3),
            _single_buffered((1, RET_WIDTH), const2),
            _single_buffered(w_out.shape, const2),
            _single_buffered((1, d), const2),
        ],
        out_specs=pl.BlockSpec((1, tb, d), lambda i, j: (i, j, 0)),
        scratch_shapes=[
            pltpu.VMEM((RET_HEADS, RET_QK, RET_V), _F32),
            pltpu.VMEM((RET_HEADS, tb, RET_QK), _F32),
            pltpu.VMEM((RET_HEADS, tb, RET_QK), _F32),
            pltpu.VMEM((tb, RET_WIDTH), _BF16),
            pltpu.VMEM((tb, RET_WIDTH), _F32),
            pltpu.VMEM((tb, RET_WIDTH), _BF16),
        ],
        compiler_params=pltpu.CompilerParams(
            dimension_semantics=("parallel", "arbitrary"),
            vmem_limit_bytes=VMEM_LIMIT_BYTES),
        name="retention_layer",
    )(x, g_pre.reshape(1, d), w_in.astype(_BF16), cos_t, sin_t, dmask, xi, zeta, g_span,
      gn_gain.reshape(1, RET_WIDTH), w_out.astype(_BF16), g_post.reshape(1, d))


def _projection_kernel(h_ref, gkv_ref, wkva_ref, gkvlat_ref, wuk_ref, wuvt_ref, gpre_ref, winb_ref,
                       gqlat_ref, wuqt_ref, cos_ref, sin_ref, cost_ref, sint_ref,
                       qt_ref, kn_ref, kr_ref, vt_ref, sg_ref):
    def store_slabs(ref, rows, value):
        for i in range(value.shape[1] // ATTN_BLOCK):
            ref[i, rows, :] = value[:, i * ATTN_BLOCK:(i + 1) * ATTN_BLOCK].astype(_BF16)

    x = h_ref[...]
    xn = x * _rms_scale(x)
    cos = cos_ref[...]
    sin = sin_ref[...]

    a = _dot((xn * gkv_ref[...]).astype(_BF16), wkva_ref[...])
    c_kv = a[:, :KV_LORA]
    kr_ref[...] = _rotate(a[:, KV_LORA:], cos, sin).astype(_BF16)
    c_kv = (c_kv * _rms_scale(c_kv) * gkvlat_ref[...]).astype(_BF16)
    kn_ref[...] = _dot(c_kv, wuk_ref[...]).astype(_BF16)
    store_slabs(vt_ref, slice(None), _dot_nt(wuvt_ref[...], c_kv))

    proj = _dot((xn * gpre_ref[...]).astype(_BF16), winb_ref[...])
    half_gate = 0.5 * proj[:, Q_LORA:]
    sg_ref[...] = (half_gate * jnp.tanh(half_gate) + half_gate).astype(_BF16)
    c_q = proj[:, :Q_LORA]
    c_q = (c_q * _rms_scale(c_q) * gqlat_ref[...]).astype(_BF16)
    qt = _dot_nt(wuqt_ref[...], c_q)
    scale = (MLA_NOPE + MLA_ROPE) ** -0.5 * LOG2_E
    half = MLA_ROPE // 2
    cos_rows = cost_ref[...]
    sin_rows = sint_ref[...]
    zero_rows = jnp.zeros((LANES // 2 - half, qt.shape[1]), _BF16)
    for h in range(MLA_HEADS):
        src = h * (MLA_NOPE + MLA_ROPE)
        dst = h * MLA_QK_PAD
        store_slabs(qt_ref, slice(dst, dst + MLA_NOPE), qt[src:src + MLA_NOPE] * scale)
        x1 = qt[src + MLA_NOPE:src + MLA_NOPE + half]
        x2 = qt[src + MLA_NOPE + half:src + MLA_NOPE + 2 * half]
        dst += MLA_NOPE
        store_slabs(qt_ref, slice(dst, dst + half), (x1 * cos_rows - x2 * sin_rows) * scale)
        store_slabs(qt_ref, slice(dst + half, dst + LANES // 2), zero_rows)
        dst += LANES // 2
        store_slabs(qt_ref, slice(dst, dst + half), (x1 * sin_rows + x2 * cos_rows) * scale)
        store_slabs(qt_ref, slice(dst + half, dst + LANES // 2), zero_rows)


def _spread_rope_columns(w):
    half = MLA_ROPE // 2
    z = jnp.zeros(w.shape[:-1] + (LANES // 2 - half,), w.dtype)
    return jnp.concatenate([w[..., :half], z, w[..., half:], z], axis=-1)


def _projections(h, g_kv, w_kv_a, g_kv_lat, w_uk, w_uv, g_pre, w_in, g_q_lat, w_uq):
    b, s, d = h.shape
    t = b * s
    tb = PROJ_TOKEN_BLOCK
    half = MLA_ROPE // 2
    cos, sin = _rope_angles(s, half)
    ones = np.ones((s, LANES // 2 - half), np.float32)
    zeros = np.zeros((s, LANES // 2 - half), np.float32)
    cos_t = np.concatenate([cos, ones, cos, ones], axis=-1)
    sin_t = np.concatenate([-sin, zeros, sin, zeros], axis=-1)

    wkva = jnp.concatenate([w_kv_a[:, :KV_LORA], _spread_rope_columns(w_kv_a[:, KV_LORA:])], axis=-1)

    const = lambda i: (0, 0)
    tok = lambda i: (i, 0)
    pos = lambda i: (i % (s // tb), 0)
    pos_t = lambda i: (0, i % (s // tb))
    weights = [
        g_kv.reshape(1, d), wkva.astype(_BF16), g_kv_lat.reshape(1, KV_LORA), w_uk.astype(_BF16),
        w_uv.T.astype(_BF16), g_pre.reshape(1, d), w_in.astype(_BF16), g_q_lat.reshape(1, Q_LORA),
        w_uq.T.astype(_BF16),
    ]

    def token_major(width):
        return jax.ShapeDtypeStruct((t, width), _BF16), pl.BlockSpec((tb, width), tok)

    def feature_major(width):
        return (jax.ShapeDtypeStruct((t // ATTN_BLOCK, width, ATTN_BLOCK), _BF16),
                pl.BlockSpec((tb // ATTN_BLOCK, width, ATTN_BLOCK), lambda i: (i, 0, 0)))

    outs = [feature_major(MLA_HEADS * MLA_QK_PAD), token_major(MLA_HEADS * MLA_NOPE),
            token_major(LANES), feature_major(MLA_WIDTH), token_major(MLA_WIDTH)]
    qt, kn, kr, vt, sg = pl.pallas_call(
        _projection_kernel,
        out_shape=[o[0] for o in outs],
        grid=(t // tb,),
        in_specs=[pl.BlockSpec((tb, d), tok)]
        + [_single_buffered(w.shape, const) for w in weights]
        + [pl.BlockSpec((tb, LANES), pos), pl.BlockSpec((tb, LANES), pos),
           pl.BlockSpec((half, tb), pos_t), pl.BlockSpec((half, tb), pos_t)],
        out_specs=[o[1] for o in outs],
        compiler_params=pltpu.CompilerParams(
            dimension_semantics=("parallel",),
            vmem_limit_bytes=VMEM_LIMIT_BYTES),
        name="latent_projections",
    )(h.reshape(t, d), *weights, cos_t, sin_t,
      np.ascontiguousarray(cos.T), np.ascontiguousarray(sin.T))
    kn, kr, sg = [o.reshape(b, s, -1) for o in (kn, kr, sg)]
    return (qt.reshape(b, s // ATTN_BLOCK, MLA_HEADS * MLA_QK_PAD, ATTN_BLOCK), kn, kr,
            vt.reshape(b, s // ATTN_BLOCK, MLA_WIDTH, ATTN_BLOCK), sg)


def _attention_kernel(qt_ref, kn_ref, kr_ref, vt_ref, sg_ref, h_ref, wout_ref, gpost_ref, o_ref,
                      m_s, den_s, acc_s, y_s):
    blk = ATTN_BLOCK
    qi = pl.program_id(1)
    key_chunk = lax.broadcasted_iota(jnp.int32, (blk, blk), 0) // CHUNK
    query_chunk = lax.broadcasted_iota(jnp.int32, (blk, blk), 1) // CHUNK
    visible = key_chunk <= query_chunk
    ones = jnp.ones((ONES_ROWS, blk), _BF16)

    heads = range(MLA_HEADS)

    def head_scores(ki):
        rows = pl.ds(pl.multiple_of(ki * blk, blk), blk)
        k_rope = kr_ref[0, rows, :]

        def score(h):
            qt = qt_ref[0, 0, h * MLA_QK_PAD:(h + 1) * MLA_QK_PAD, :]
            k = jnp.concatenate([kn_ref[0, rows, h * MLA_NOPE:(h + 1) * MLA_NOPE], k_rope], axis=-1)
            return _dot(k, qt)

        scores = [score(h) for h in heads[:SCORE_LOOKAHEAD]]
        for h in heads:
            if h + SCORE_LOOKAHEAD < MLA_HEADS:
                scores.append(score(h + SCORE_LOOKAHEAD))
            yield h, scores[h]

    def values(ki, h):
        return jnp.concatenate([vt_ref[0, ki, h * MLA_V:(h + 1) * MLA_V, :], ones], axis=0)

    def running_max_step(ki, carry=None):
        seen = jnp.logical_or(visible, ki < qi)
        for h, st in head_scores(ki):
            st = jnp.where(seen, st, -jnp.inf)
            m_prev = m_s[h]
            m_new = jnp.maximum(m_prev, jnp.max(st, axis=0, keepdims=True))
            pt = jnp.exp2(st - m_new).astype(_BF16)
            alpha = jnp.exp2(m_prev - m_new)
            acc_s[h] = alpha * acc_s[h] + _dot(values(ki, h), pt)
            m_s[h] = m_new
        return carry

    def fixed_reference_first_step(last):
        for h, st in head_scores(0):
            if last:
                st = jnp.where(visible, st, -jnp.inf)
            m_ref = jnp.max(st[0:CHUNK], axis=0, keepdims=True)
            m_s[h] = m_ref
            pt = jnp.exp2(st - m_ref).astype(_BF16)
            acc = _dot(values(0, h), pt)
            if last:
                finish_head(h, acc)
            else:
                acc_s[h] = acc

    def finish_head(h, acc):
        head = slice(h * MLA_V, (h + 1) * MLA_V)
        den_s[h] = acc[MLA_V:MLA_V + 1, :]
        o = (acc[0:MLA_V, :] / acc[MLA_V:MLA_V + 1, :]).T
        y_s[:, head] = (sg_ref[0, :, head].astype(_F32) * o).astype(_BF16)

    def finish():
        for h in heads:
            finish_head(h, acc_s[h])

    def fixed_reference_step(ki, last):
        for h, st in head_scores(ki):
            if last:
                st = jnp.where(visible, st, -jnp.inf)
            pt = jnp.exp2(st - m_s[h]).astype(_BF16)
            acc = acc_s[h] + _dot(values(ki, h), pt)
            if last:
                finish_head(h, acc)
            else:
                acc_s[h] = acc

    def reset():
        m_s[...] = jnp.full_like(m_s, -jnp.inf)
        acc_s[...] = jnp.zeros_like(acc_s)

    @pl.when(qi == 0)
    def _():
        fixed_reference_first_step(last=True)

    @pl.when(qi > 0)
    def _():
        fixed_reference_first_step(last=False)
        lax.fori_loop(1, qi, lambda ki, c: (fixed_reference_step(ki, last=False), c)[1], 0)
        fixed_reference_step(qi, last=True)

    in_range = jnp.max(den_s[...]) <= MAX_DENOMINATOR

    def write_output():
        y = _dot(y_s[...], wout_ref[...])
        o_ref[0] = h_ref[0] + y * _rms_scale(y) * gpost_ref[...]

    write_output()

    @pl.when(jnp.logical_not(in_range))
    def _():
        reset()
        lax.fori_loop(0, qi + 1, running_max_step, 0)
        finish()
        write_output()


def _attention_layer(h, qt, kn, kr, vt, sg, w_out, g_post):
    b, s, d = h.shape
    blk = ATTN_BLOCK
    assert vt.shape == (b, s // blk, MLA_WIDTH, blk)
    assert qt.shape == (b, s // blk, MLA_HEADS * MLA_QK_PAD, blk)
    whole = lambda i, j: (i, 0, 0)
    qblk = lambda i, j: (i, j, 0)
    const = lambda i, j: (0, 0)
    return pl.pallas_call(
        _attention_kernel,
        out_shape=jax.ShapeDtypeStruct((b, s, d), _F32),
        grid=(b, s // blk),
        in_specs=[
            pl.BlockSpec((1, 1) + qt.shape[2:], lambda i, j: (i, j, 0, 0)),
            pl.BlockSpec((1, s, kn.shape[-1]), whole),
            pl.BlockSpec((1, s, kr.shape[-1]), whole),
            pl.BlockSpec((1,) + vt.shape[1:], lambda i, j: (i, 0, 0, 0)),
            pl.BlockSpec((1, blk, MLA_WIDTH), qblk),
            pl.BlockSpec((1, blk, d), qblk),
            _single_buffered(w_out.shape, const),
            _single_buffered((1, d), const),
        ],
        out_specs=pl.BlockSpec((1, blk, d), qblk),
        scratch_shapes=[
            pltpu.VMEM((MLA_HEADS, 1, blk), _F32),
            pltpu.VMEM((MLA_HEADS, 1, blk), _F32),
            pltpu.VMEM((MLA_HEADS, MLA_V + ONES_ROWS, blk), _F32),
            pltpu.VMEM((blk, MLA_WIDTH), _BF16),
        ],
        compiler_params=pltpu.CompilerParams(
            dimension_semantics=("parallel", "arbitrary"),
            vmem_limit_bytes=VMEM_LIMIT_BYTES),
        name="latent_attention",
    )(qt, kn, kr, vt, sg, h, w_out.astype(_BF16), g_post.reshape(1, d))


@jax.jit
def kernel(x, g_pre_a, w_in_a, gn_gain_a, w_out_a, g_post_a, g_kv, w_kv_a, g_kv_lat, w_uk, w_uv,
           g_pre_b, w_in_b, g_q_lat, w_uq, w_out_b, g_post_b):
    assert w_in_a.shape[0] == 1 and w_in_b.shape[0] == 1
    h = _retention_layer(x, g_pre_a[0], w_in_a[0], gn_gain_a[0], w_out_a[0], g_post_a[0])
    qt, kn, kr, vt, sg = _projections(h, g_kv, w_kv_a, g_kv_lat, w_uk, w_uv,
                                      g_pre_b[0], w_in_b[0], g_q_lat[0], w_uq[0])
    return _attention_layer(h, qt, kn, kr, vt, sg, w_out_b[0], g_post_b[0])
```

```python
import jax
import jax.numpy as jnp
import numpy as np
from jax import lax
from jax.experimental import pallas as pl
from jax.experimental.pallas import tpu as pltpu

D_MODEL = 1024
CHUNK = 64
RMS_EPS = 1e-6
ROPE_BASE = 10000.0

RET_HEADS = 8
RET_QK = 128
RET_V = 256
RET_QK_WIDTH = RET_HEADS * RET_QK
RET_WIDTH = RET_HEADS * RET_V

MLA_HEADS = 8
MLA_NOPE = 128
MLA_ROPE = 64
MLA_V = 128
MLA_WIDTH = MLA_HEADS * MLA_V
Q_LORA = 384
KV_LORA = 256

LANES = 128
MLA_QK_PAD = 2 * LANES
VMEM_BYTES_V7X = 64 * 1024 * 1024
VMEM_LIMIT_BYTES = VMEM_BYTES_V7X - 4 * 1024 * 1024

RET_TOKEN_BLOCK = 512
RET_SPAN = 256
PROJ_TOKEN_BLOCK = 1024
ATTN_BLOCK = 512
SCORE_LOOKAHEAD = 2
ONES_ROWS = 16
MAX_DENOMINATOR = 2.0 ** 24

LOG2_E = 1.4426950408889634

_BF16 = jnp.bfloat16
_F32 = jnp.float32


def _single_buffered(shape, index_map):
    return pl.BlockSpec(shape, index_map, pipeline_mode=pl.Buffered(1))


def _rms_scale(x):
    return lax.rsqrt(jnp.mean(x * x, axis=-1, keepdims=True) + RMS_EPS)


def _rotate(x, cos, sin_signed):
    return x * cos + pltpu.roll(x, LANES // 2, axis=1) * sin_signed


def _dot(a, b):
    return jnp.dot(a, b, preferred_element_type=_F32)


def _dot_nt(a, b):
    return lax.dot_general(a, b, (((1,), (1,)), ((), ())), preferred_element_type=_F32)


def _dot_tn(a, b):
    return lax.dot_general(a, b, (((0,), (0,)), ((), ())), preferred_element_type=_F32)


def _retention_kernel(x_ref, gpre_ref, win_ref, cos_ref, sin_ref, dmask_ref, xi_ref, zeta_ref,
                      gspan_ref, gn_ref, wout_ref, gpost_ref, o_ref,
                      state_ref, q_s, k_s, v_s, gate_s, y_s):
    @pl.when(pl.program_id(1) == 0)
    def _():
        state_ref[...] = jnp.zeros_like(state_ref)

    x = x_ref[0]
    hb = (x * _rms_scale(x) * gpre_ref[...]).astype(_BF16)
    cos = cos_ref[...]
    sin = sin_ref[...]

    q_all = _dot(hb, win_ref[:, 0:RET_QK_WIDTH])
    for h in range(RET_HEADS):
        q_s[h] = _rotate(q_all[:, h * RET_QK:(h + 1) * RET_QK], cos, sin)
    k_all = _dot(hb, win_ref[:, RET_QK_WIDTH:2 * RET_QK_WIDTH])
    for h in range(RET_HEADS):
        k_s[h] = _rotate(k_all[:, h * RET_QK:(h + 1) * RET_QK], cos, sin)
    v_s[...] = _dot(hb, win_ref[:, 2 * RET_QK_WIDTH:2 * RET_QK_WIDTH + RET_WIDTH]).astype(_BF16)
    gate_s[...] = _dot(hb, win_ref[:, 2 * RET_QK_WIDTH + RET_WIDTH:])

    for s in range(RET_TOKEN_BLOCK // RET_SPAN):
        rows = slice(s * RET_SPAN, (s + 1) * RET_SPAN)
        for h in range(RET_HEADS):
            cols = slice(h * RET_V, (h + 1) * RET_V)
            q = q_s[h, rows, :]
            k = k_s[h, rows, :]
            v = v_s[rows, cols]
            scores = _dot_nt(q.astype(_BF16), k.astype(_BF16)) * dmask_ref[h]
            state = state_ref[h]
            o = _dot(scores.astype(_BF16), v)
            o = o + _dot((q * xi_ref[h]).astype(_BF16), state.astype(_BF16))
            state_ref[h] = state * gspan_ref[h] + _dot_tn((k * zeta_ref[h]).astype(_BF16), v)
            mu = jnp.mean(o, axis=-1, keepdims=True)
            oc = o - mu
            var = jnp.mean(oc * oc, axis=-1, keepdims=True)
            on = oc * lax.rsqrt(var + RMS_EPS) * gn_ref[:, cols]
            half_gate = 0.5 * gate_s[rows, cols]
            silu = half_gate * jnp.tanh(half_gate) + half_gate
            y_s[rows, cols] = (silu * on).astype(_BF16)

    y = _dot(y_s[...], wout_ref[...])
    o_ref[0] = x + y * _rms_scale(y) * gpost_ref[...]


def _retention_tables():
    log_g = np.log(1.0 - np.exp2(-5.0 - np.arange(RET_HEADS, dtype=np.float64)))
    idx = np.arange(RET_SPAN, dtype=np.float64)
    diff = idx[:, None] - idx[None, :]
    chunk = np.arange(RET_SPAN) // CHUNK
    same = chunk[:, None] == chunk[None, :]
    earlier = chunk[None, :] < chunk[:, None]
    dist = np.where(same, np.abs(diff), diff)
    decay = np.exp(log_g[:, None, None] * np.where(same | earlier, dist, 0.0)[None])
    k_scale = RET_QK ** -0.5
    dmask = np.where((same | earlier)[None], decay, 0.0) * k_scale
    xi = np.exp(log_g[:, None] * (idx + 1.0)[None]) * k_scale
    zeta = np.exp(log_g[:, None] * (RET_SPAN - 1.0 - idx)[None])
    g_span = np.exp(log_g * RET_SPAN)
    xi = np.broadcast_to(xi[:, :, None], (RET_HEADS, RET_SPAN, RET_QK))
    zeta = np.broadcast_to(zeta[:, :, None], (RET_HEADS, RET_SPAN, RET_QK))
    g_span = np.broadcast_to(g_span[:, None, None], (RET_HEADS, 1, RET_V))
    return tuple(np.ascontiguousarray(t, dtype=np.float32) for t in (dmask, xi, zeta, g_span))


def _rope_angles(seq, half):
    inv = ROPE_BASE ** (-np.arange(half, dtype=np.float64) / half)
    ang = np.arange(seq, dtype=np.float64)[:, None] * inv[None, :]
    return np.cos(ang).astype(np.float32), np.sin(ang).astype(np.float32)


def _retention_layer(x, g_pre, w_in, gn_gain, w_out, g_post):
    b, s, d = x.shape
    tb = RET_TOKEN_BLOCK
    cos, sin = _rope_angles(s, RET_QK // 2)
    cos_t = np.concatenate([cos, cos], axis=-1)
    sin_t = np.concatenate([-sin, sin], axis=-1)
    dmask, xi, zeta, g_span = _retention_tables()
    const2 = lambda i, j: (0, 0)
    const3 = lambda i, j: (0, 0, 0)
    return pl.pallas_call(
        _retention_kernel,
        out_shape=jax.ShapeDtypeStruct((b, s, d), _F32),
        grid=(b, s // tb),
        in_specs=[
            pl.BlockSpec((1, tb, d), lambda i, j: (i, j, 0)),
            _single_buffered((1, d), const2),
            _single_buffered(w_in.shape, const2),
            pl.BlockSpec((tb, RET_QK), lambda i, j: (j, 0)),
            pl.BlockSpec((tb, RET_QK), lambda i, j: (j, 0)),
            _single_buffered(dmask.shape, const3),
            _single_buffered(xi.shape, const3),
            _single_buffered(zeta.shape, const3),
            _single_buffered(g_span.shape, const3),
            _single_buffered((1, RET_WIDTH), const2),
            _single_buffered(w_out.shape, const2),
            _single_buffered((1, d), const2),
        ],
        out_specs=pl.BlockSpec((1, tb, d), lambda i, j: (i, j, 0)),
        scratch_shapes=[
            pltpu.VMEM((RET_HEADS, RET_QK, RET_V), _F32),
            pltpu.VMEM((RET_HEADS, tb, RET_QK), _F32),
            pltpu.VMEM((RET_HEADS, tb, RET_QK), _F32),
            pltpu.VMEM((tb, RET_WIDTH), _BF16),
            pltpu.VMEM((tb, RET_WIDTH), _F32),
            pltpu.VMEM((tb, RET_WIDTH), _BF16),
        ],
        compiler_params=pltpu.CompilerParams(
            dimension_semantics=("parallel", "arbitrary"),
            vmem_limit_bytes=VMEM_LIMIT_BYTES),
        name="retention_layer",
    )(x, g_pre.reshape(1, d), w_in.astype(_BF16), cos_t, sin_t, dmask, xi, zeta, g_span,
      gn_gain.reshape(1, RET_WIDTH), w_out.astype(_BF16), g_post.reshape(1, d))


def _projection_kernel(h_ref, gkv_ref, wkva_ref, gkvlat_ref, wuk_ref, wuvt_ref, gpre_ref, winb_ref,
                       gqlat_ref, wuqt_ref, cos_ref, sin_ref, cost_ref, sint_ref,
                       qt_ref, kn_ref, kr_ref, vt_ref, sg_ref, wcat_s):
    def store_slabs(ref, rows, value):
        for i in range(value.shape[1] // ATTN_BLOCK):
            ref[i, rows, :] = value[:, i * ATTN_BLOCK:(i + 1) * ATTN_BLOCK].astype(_BF16)

    n_kv = wkva_ref.shape[1]

    @pl.when(pl.program_id(0) == 0)
    def _():
        wcat_s[:, :n_kv] = (wkva_ref[...] * gkv_ref[...]).astype(_BF16)
        wcat_s[:, n_kv:] = (winb_ref[...] * gpre_ref[...]).astype(_BF16)

    x = h_ref[...]
    xn = x * _rms_scale(x)
    cos = cos_ref[...]
    sin = sin_ref[...]

    both = _dot(xn.astype(_BF16), wcat_s[...])
    a = both[:, :n_kv]
    c_kv = a[:, :KV_LORA]
    kr_ref[...] = _rotate(a[:, KV_LORA:], cos, sin).astype(_BF16)
    c_kv = (c_kv * _rms_scale(c_kv) * gkvlat_ref[...]).astype(_BF16)
    kn_ref[...] = _dot(c_kv, wuk_ref[...]).astype(_BF16)
    store_slabs(vt_ref, slice(None), _dot_nt(wuvt_ref[...], c_kv))

    proj = both[:, n_kv:]
    half_gate = 0.5 * proj[:, Q_LORA:]
    sg_ref[...] = (half_gate * jnp.tanh(half_gate) + half_gate).astype(_BF16)
    c_q = proj[:, :Q_LORA]
    c_q = (c_q * _rms_scale(c_q) * gqlat_ref[...]).astype(_BF16)
    qt = _dot_nt(wuqt_ref[...], c_q)
    scale = (MLA_NOPE + MLA_ROPE) ** -0.5 * LOG2_E
    half = MLA_ROPE // 2
    cos_rows = cost_ref[...]
    sin_rows = sint_ref[...]
    zero_rows = jnp.zeros((LANES // 2 - half, qt.shape[1]), _BF16)
    for h in range(MLA_HEADS):
        src = h * (MLA_NOPE + MLA_ROPE)
        dst = h * MLA_QK_PAD
        store_slabs(qt_ref, slice(dst, dst + MLA_NOPE), qt[src:src + MLA_NOPE] * scale)
        x1 = qt[src + MLA_NOPE:src + MLA_NOPE + half]
        x2 = qt[src + MLA_NOPE + half:src + MLA_NOPE + 2 * half]
        dst += MLA_NOPE
        store_slabs(qt_ref, slice(dst, dst + half), (x1 * cos_rows - x2 * sin_rows) * scale)
        store_slabs(qt_ref, slice(dst + half, dst + LANES // 2), zero_rows)
        dst += LANES // 2
        store_slabs(qt_ref, slice(dst, dst + half), (x1 * sin_rows + x2 * cos_rows) * scale)
        store_slabs(qt_ref, slice(dst + half, dst + LANES // 2), zero_rows)


def _spread_rope_columns(w):
    half = MLA_ROPE // 2
    z = jnp.zeros(w.shape[:-1] + (LANES // 2 - half,), w.dtype)
    return jnp.concatenate([w[..., :half], z, w[..., half:], z], axis=-1)


def _projections(h, g_kv, w_kv_a, g_kv_lat, w_uk, w_uv, g_pre, w_in, g_q_lat, w_uq):
    b, s, d = h.shape
    t = b * s
    tb = PROJ_TOKEN_BLOCK
    half = MLA_ROPE // 2
    cos, sin = _rope_angles(s, half)
    ones = np.ones((s, LANES // 2 - half), np.float32)
    zeros = np.zeros((s, LANES // 2 - half), np.float32)
    cos_t = np.concatenate([cos, ones, cos, ones], axis=-1)
    sin_t = np.concatenate([-sin, zeros, sin, zeros], axis=-1)

    wkva = jnp.concatenate([w_kv_a[:, :KV_LORA], _spread_rope_columns(w_kv_a[:, KV_LORA:])], axis=-1)

    const = lambda i: (0, 0)
    tok = lambda i: (i, 0)
    pos = lambda i: (i % (s // tb), 0)
    pos_t = lambda i: (0, i % (s // tb))
    weights = [
        g_kv.reshape(d, 1), wkva, g_kv_lat.reshape(1, KV_LORA), w_uk.astype(_BF16),
        w_uv.T.astype(_BF16), g_pre.reshape(d, 1), w_in, g_q_lat.reshape(1, Q_LORA),
        w_uq.T.astype(_BF16),
    ]

    def token_major(width):
        return jax.ShapeDtypeStruct((t, width), _BF16), pl.BlockSpec((tb, width), tok)

    def feature_major(width):
        return (jax.ShapeDtypeStruct((t // ATTN_BLOCK, width, ATTN_BLOCK), _BF16),
                pl.BlockSpec((tb // ATTN_BLOCK, width, ATTN_BLOCK), lambda i: (i, 0, 0)))

    outs = [feature_major(MLA_HEADS * MLA_QK_PAD), token_major(MLA_HEADS * MLA_NOPE),
            token_major(LANES), feature_major(MLA_WIDTH), token_major(MLA_WIDTH)]
    qt, kn, kr, vt, sg = pl.pallas_call(
        _projection_kernel,
        out_shape=[o[0] for o in outs],
        grid=(t // tb,),
        in_specs=[pl.BlockSpec((tb, d), tok)]
        + [_single_buffered(w.shape, const) for w in weights]
        + [pl.BlockSpec((tb, LANES), pos), pl.BlockSpec((tb, LANES), pos),
           pl.BlockSpec((half, tb), pos_t), pl.BlockSpec((half, tb), pos_t)],
        out_specs=[o[1] for o in outs],
        scratch_shapes=[pltpu.VMEM((d, wkva.shape[1] + w_in.shape[1]), _BF16)],
        compiler_params=pltpu.CompilerParams(
            dimension_semantics=("arbitrary",),
            vmem_limit_bytes=VMEM_LIMIT_BYTES),
        name="latent_projections",
    )(h.reshape(t, d), *weights, cos_t, sin_t,
      np.ascontiguousarray(cos.T), np.ascontiguousarray(sin.T))
    kn, kr, sg = [o.reshape(b, s, -1) for o in (kn, kr, sg)]
    return (qt.reshape(b, s // ATTN_BLOCK, MLA_HEADS * MLA_QK_PAD, ATTN_BLOCK), kn, kr,
            vt.reshape(b, s // ATTN_BLOCK, MLA_WIDTH, ATTN_BLOCK), sg)


def _attention_kernel(qt_ref, kn_ref, kr_ref, vt_ref, sg_ref, h_ref, wout_ref, gpost_ref, o_ref,
                      m_s, den_s, acc_s, y_s):
    blk = ATTN_BLOCK
    qi = pl.program_id(1)
    key_chunk = lax.broadcasted_iota(jnp.int32, (blk, blk), 0) // CHUNK
    query_chunk = lax.broadcasted_iota(jnp.int32, (blk, blk), 1) // CHUNK
    visible = key_chunk <= query_chunk
    ones = jnp.ones((ONES_ROWS, blk), _BF16)

    heads = range(MLA_HEADS)

    def head_scores(ki):
        rows = pl.ds(pl.multiple_of(ki * blk, blk), blk)
        k_rope = kr_ref[0, rows, :]

        def score(h):
            qt = qt_ref[0, 0, h * MLA_QK_PAD:(h + 1) * MLA_QK_PAD, :]
            k = jnp.concatenate([kn_ref[0, rows, h * MLA_NOPE:(h + 1) * MLA_NOPE], k_rope], axis=-1)
            return _dot(k, qt)

        scores = [score(h) for h in heads[:SCORE_LOOKAHEAD]]
        for h in heads:
            if h + SCORE_LOOKAHEAD < MLA_HEADS:
                scores.append(score(h + SCORE_LOOKAHEAD))
            yield h, scores[h]

    def values(ki, h):
        return jnp.concatenate([vt_ref[0, ki, h * MLA_V:(h + 1) * MLA_V, :], ones], axis=0)

    def running_max_step(ki, carry=None):
        seen = jnp.logical_or(visible, ki < qi)
        for h, st in head_scores(ki):
            st = jnp.where(seen, st, -jnp.inf)
            m_prev = m_s[h]
            m_new = jnp.maximum(m_prev, jnp.max(st, axis=0, keepdims=True))
            pt = jnp.exp2(st - m_new).astype(_BF16)
            alpha = jnp.exp2(m_prev - m_new)
            acc_s[h] = alpha * acc_s[h] + _dot(values(ki, h), pt)
            m_s[h] = m_new
        return carry

    def fixed_reference_first_step(last):
        for h, st in head_scores(0):
            if last:
                st = jnp.where(visible, st, -jnp.inf)
            m_ref = jnp.max(st[0:CHUNK], axis=0, keepdims=True)
            m_s[h] = m_ref
            pt = jnp.exp2(st - m_ref).astype(_BF16)
            acc = _dot(values(0, h), pt)
            if last:
                finish_head(h, acc)
            else:
                acc_s[h] = acc

    def finish_head(h, acc):
        head = slice(h * MLA_V, (h + 1) * MLA_V)
        den_s[h] = acc[MLA_V:MLA_V + 1, :]
        o = (acc[0:MLA_V, :] / acc[MLA_V:MLA_V + 1, :]).T
        y_s[:, head] = (sg_ref[0, :, head].astype(_F32) * o).astype(_BF16)

    def finish():
        for h in heads:
            finish_head(h, acc_s[h])

    def fixed_reference_step(ki, last):
        for h, st in head_scores(ki):
            if last:
                st = jnp.where(visible, st, -jnp.inf)
            pt = jnp.exp2(st - m_s[h]).astype(_BF16)
            acc = acc_s[h] + _dot(values(ki, h), pt)
            if last:
                finish_head(h, acc)
            else:
                acc_s[h] = acc

    def reset():
        m_s[...] = jnp.full_like(m_s, -jnp.inf)
        acc_s[...] = jnp.zeros_like(acc_s)

    @pl.when(qi == 0)
    def _():
        fixed_reference_first_step(last=True)

    @pl.when(qi > 0)
    def _():
        fixed_reference_first_step(last=False)
        lax.fori_loop(1, qi, lambda ki, c: (fixed_reference_step(ki, last=False), c)[1], 0)
        fixed_reference_step(qi, last=True)

    in_range = jnp.max(den_s[...]) <= MAX_DENOMINATOR

    @pl.when(jnp.logical_not(in_range))
    def _():
        reset()
        lax.fori_loop(0, qi + 1, running_max_step, 0)
        finish()

    y = _dot(y_s[...], wout_ref[...])
    o_ref[0] = h_ref[0] + y * _rms_scale(y) * gpost_ref[...]


def _attention_layer(h, qt, kn, kr, vt, sg, w_out, g_post):
    b, s, d = h.shape
    blk = ATTN_BLOCK
    assert vt.shape == (b, s // blk, MLA_WIDTH, blk)
    assert qt.shape == (b, s // blk, MLA_HEADS * MLA_QK_PAD, blk)
    whole = lambda i, j: (i, 0, 0)
    qblk = lambda i, j: (i, j, 0)
    const = lambda i, j: (0, 0)
    return pl.pallas_call(
        _attention_kernel,
        out_shape=jax.ShapeDtypeStruct((b, s, d), _F32),
        grid=(b, s // blk),
        in_specs=[
            pl.BlockSpec((1, 1) + qt.shape[2:], lambda i, j: (i, j, 0, 0)),
            pl.BlockSpec((1, s, kn.shape[-1]), whole),
            pl.BlockSpec((1, s, kr.shape[-1]), whole),
            pl.BlockSpec((1,) + vt.shape[1:], lambda i, j: (i, 0, 0, 0)),
            pl.BlockSpec((1, blk, MLA_WIDTH), qblk),
            pl.BlockSpec((1, blk, d), qblk),
            _single_buffered(w_out.shape, const),
            _single_buffered((1, d), const),
        ],
        out_specs=pl.BlockSpec((1, blk, d), qblk),
        scratch_shapes=[
            pltpu.VMEM((MLA_HEADS, 1, blk), _F32),
            pltpu.VMEM((MLA_HEADS, 1, blk), _F32),
            pltpu.VMEM((MLA_HEADS, MLA_V + ONES_ROWS, blk), _F32),
            pltpu.VMEM((blk, MLA_WIDTH), _BF16),
        ],
        compiler_params=pltpu.CompilerParams(
            dimension_semantics=("parallel", "arbitrary"),
            vmem_limit_bytes=VMEM_LIMIT_BYTES),
        name="latent_attention",
    )(qt, kn, kr, vt, sg, h, w_out.astype(_BF16), g_post.reshape(1, d))


@jax.jit
def kernel(x, g_pre_a, w_in_a, gn_gain_a, w_out_a, g_post_a, g_kv, w_kv_a, g_kv_lat, w_uk, w_uv,
           g_pre_b, w_in_b, g_q_lat, w_uq, w_out_b, g_post_b):
    assert w_in_a.shape[0] == 1 and w_in_b.shape[0] == 1
    h = _retention_layer(x, g_pre_a[0], w_in_a[0], gn_gain_a[0], w_out_a[0], g_post_a[0])
    qt, kn, kr, vt, sg = _projections(h, g_kv, w_kv_a, g_kv_lat, w_uk, w_uv,
                                      g_pre_b[0], w_in_b[0], g_q_lat[0], w_uq[0])
    return _attention_layer(h, qt, kn, kr, vt, sg, w_out_b[0], g_post_b[0])
```

```python
import jax
import jax.numpy as jnp
import numpy as np
from jax import lax
from jax.experimental import pallas as pl
from jax.experimental.pallas import tpu as pltpu

D_MODEL = 1024
CHUNK = 64
RMS_EPS = 1e-6
ROPE_BASE = 10000.0

RET_HEADS = 8
RET_QK = 128
RET_V = 256
RET_QK_WIDTH = RET_HEADS * RET_QK
RET_WIDTH = RET_HEADS * RET_V

MLA_HEADS = 8
MLA_NOPE = 128
MLA_ROPE = 64
MLA_V = 128
MLA_WIDTH = MLA_HEADS * MLA_V
Q_LORA = 384
KV_LORA = 256

LANES = 128
MLA_QK_PAD = 2 * LANES
VMEM_BYTES_V7X = 64 * 1024 * 1024
VMEM_LIMIT_BYTES = VMEM_BYTES_V7X - 4 * 1024 * 1024

RET_TOKEN_BLOCK = 512
RET_SPAN = 256
PROJ_TOKEN_BLOCK = 1024
ATTN_BLOCK = 512
SCORE_LOOKAHEAD = 2
ONES_ROWS = 16
MAX_DENOMINATOR = 2.0 ** 24

LOG2_E = 1.4426950408889634

_BF16 = jnp.bfloat16
_F32 = jnp.float32


def _single_buffered(shape, index_map):
    return pl.BlockSpec(shape, index_map, pipeline_mode=pl.Buffered(1))


def _rms_scale(x):
    return lax.rsqrt(jnp.mean(x * x, axis=-1, keepdims=True) + RMS_EPS)


def _rotate(x, cos, sin_signed):
    return x * cos + pltpu.roll(x, LANES // 2, axis=1) * sin_signed


def _dot(a, b):
    return jnp.dot(a, b, preferred_element_type=_F32)


def _dot_nt(a, b):
    return lax.dot_general(a, b, (((1,), (1,)), ((), ())), preferred_element_type=_F32)


def _dot_tn(a, b):
    return lax.dot_general(a, b, (((0,), (0,)), ((), ())), preferred_element_type=_F32)


def _retention_kernel(x_ref, gpre_ref, win_ref, cos_ref, sin_ref, dmask_ref, xi_ref, zeta_ref,
                      gspan_ref, gn_ref, wout_ref, gpost_ref, o_ref,
                      state_ref, q_s, k_s, v_s, gate_s, y_s):
    @pl.when(pl.program_id(1) == 0)
    def _():
        state_ref[...] = jnp.zeros_like(state_ref)

    x = x_ref[0]
    hb = (x * _rms_scale(x) * gpre_ref[...]).astype(_BF16)
    cos = cos_ref[...]
    sin = sin_ref[...]

    q_all = _dot(hb, win_ref[:, 0:RET_QK_WIDTH])
    for h in range(RET_HEADS):
        q_s[h] = _rotate(q_all[:, h * RET_QK:(h + 1) * RET_QK], cos, sin)
    k_all = _dot(hb, win_ref[:, RET_QK_WIDTH:2 * RET_QK_WIDTH])
    for h in range(RET_HEADS):
        k_s[h] = _rotate(k_all[:, h * RET_QK:(h + 1) * RET_QK], cos, sin)
    v_s[...] = _dot(hb, win_ref[:, 2 * RET_QK_WIDTH:2 * RET_QK_WIDTH + RET_WIDTH]).astype(_BF16)
    gate_s[...] = _dot(hb, win_ref[:, 2 * RET_QK_WIDTH + RET_WIDTH:])

    for s in range(RET_TOKEN_BLOCK // RET_SPAN):
        rows = slice(s * RET_SPAN, (s + 1) * RET_SPAN)
        for h in range(RET_HEADS):
            cols = slice(h * RET_V, (h + 1) * RET_V)
            q = q_s[h, rows, :]
            k = k_s[h, rows, :]
            v = v_s[rows, cols]
            scores = _dot_nt(q.astype(_BF16), k.astype(_BF16)) * dmask_ref[h]
            state = state_ref[h]
            o = _dot(scores.astype(_BF16), v)
            o = o + _dot((q * xi_ref[h]).astype(_BF16), state.astype(_BF16))
            state_ref[h] = state * gspan_ref[h] + _dot_tn((k * zeta_ref[h]).astype(_BF16), v)
            mu = jnp.mean(o, axis=-1, keepdims=True)
            oc = o - mu
            var = jnp.mean(oc * oc, axis=-1, keepdims=True)
            on = oc * lax.rsqrt(var + RMS_EPS) * gn_ref[:, cols]
            half_gate = 0.5 * gate_s[rows, cols]
            silu = half_gate * jnp.tanh(half_gate) + half_gate
            y_s[rows, cols] = (silu * on).astype(_BF16)

    y = _dot(y_s[...], wout_ref[...])
    o_ref[0] = x + y * _rms_scale(y) * gpost_ref[...]


def _retention_tables():
    log_g = np.log(1.0 - np.exp2(-5.0 - np.arange(RET_HEADS, dtype=np.float64)))
    idx = np.arange(RET_SPAN, dtype=np.float64)
    diff = idx[:, None] - idx[None, :]
    chunk = np.arange(RET_SPAN) // CHUNK
    same = chunk[:, None] == chunk[None, :]
    earlier = chunk[None, :] < chunk[:, None]
    dist = np.where(same, np.abs(diff), diff)
    decay = np.exp(log_g[:, None, None] * np.where(same | earlier, dist, 0.0)[None])
    k_scale = RET_QK ** -0.5
    dmask = np.where((same | earlier)[None], decay, 0.0) * k_scale
    xi = np.exp(log_g[:, None] * (idx + 1.0)[None]) * k_scale
    zeta = np.exp(log_g[:, None] * (RET_SPAN - 1.0 - idx)[None])
    g_span = np.exp(log_g * RET_SPAN)
    xi = np.broadcast_to(xi[:, :, None], (RET_HEADS, RET_SPAN, RET_QK))
    zeta = np.broadcast_to(zeta[:, :, None], (RET_HEADS, RET_SPAN, RET_QK))
    g_span = np.broadcast_to(g_span[:, None, None], (RET_HEADS, 1, RET_V))
    return tuple(np.ascontiguousarray(t, dtype=np.float32) for t in (dmask, xi, zeta, g_span))


def _rope_angles(seq, half):
    inv = ROPE_BASE ** (-np.arange(half, dtype=np.float64) / half)
    ang = np.arange(seq, dtype=np.float64)[:, None] * inv[None, :]
    return np.cos(ang).astype(np.float32), np.sin(ang).astype(np.float32)


def _retention_layer(x, g_pre, w_in, gn_gain, w_out, g_post):
    b, s, d = x.shape
    tb = RET_TOKEN_BLOCK
    cos, sin = _rope_angles(s, RET_QK // 2)
    cos_t = np.concatenate([cos, cos], axis=-1)
    sin_t = np.concatenate([-sin, sin], axis=-1)
    dmask, xi, zeta, g_span = _retention_tables()
    const2 = lambda i, j: (0, 0)
    const3 = lambda i, j: (0, 0, 0)
    return pl.pallas_call(
        _retention_kernel,
        out_shape=jax.ShapeDtypeStruct((b, s, d), _F32),
        grid=(b, s // tb),
        in_specs=[
            pl.BlockSpec((1, tb, d), lambda i, j: (i, j, 0)),
            _single_buffered((1, d), const2),
            _single_buffered(w_in.shape, const2),
            pl.BlockSpec((tb, RET_QK), lambda i, j: (j, 0)),
            pl.BlockSpec((tb, RET_QK), lambda i, j: (j, 0)),
            _single_buffered(dmask.shape, const3),
            _single_buffered(xi.shape, const3),
            _single_buffered(zeta.shape, const3),
            _single_buffered(g_span.shape, const3),
            _single_buffered((1, RET_WIDTH), const2),
            _single_buffered(w_out.shape, const2),
            _single_buffered((1, d), const2),
        ],
        out_specs=pl.BlockSpec((1, tb, d), lambda i, j: (i, j, 0)),
        scratch_shapes=[
            pltpu.VMEM((RET_HEADS, RET_QK, RET_V), _F32),
            pltpu.VMEM((RET_HEADS, tb, RET_QK), _F32),
            pltpu.VMEM((RET_HEADS, tb, RET_QK), _F32),
            pltpu.VMEM((tb, RET_WIDTH), _BF16),
            pltpu.VMEM((tb, RET_WIDTH), _F32),
            pltpu.VMEM((tb, RET_WIDTH), _BF16),
        ],
        compiler_params=pltpu.CompilerParams(
            dimension_semantics=("parallel", "arbitrary"),
            vmem_limit_bytes=VMEM_LIMIT_BYTES),
        name="retention_layer",
    )(x, g_pre.reshape(1, d), w_in.astype(_BF16), cos_t, sin_t, dmask, xi, zeta, g_span,
      gn_gain.reshape(1, RET_WIDTH), w_out.astype(_BF16), g_post.reshape(1, d))


def _projection_kernel(h_ref, gkv_ref, wkva_ref, gkvlat_ref, wuk_ref, wuvt_ref, gpre_ref, winb_ref,
                       gqlat_ref, wuqt_ref, cos_ref, sin_ref, cost_ref, sint_ref,
                       qt_ref, kn_ref, kr_ref, vt_ref, sg_ref, wcat_s):
    def store_slabs(ref, rows, value):
        for i in range(value.shape[1] // ATTN_BLOCK):
            ref[i, rows, :] = value[:, i * ATTN_BLOCK:(i + 1) * ATTN_BLOCK].astype(_BF16)

    n_kv = wkva_ref.shape[1]

    @pl.when(pl.program_id(0) == 0)
    def _():
        wcat_s[:, :n_kv] = (wkva_ref[...] * gkv_ref[...]).astype(_BF16)
        wcat_s[:, n_kv:] = (winb_ref[...] * gpre_ref[...]).astype(_BF16)

    x = h_ref[...]
    xn = x * _rms_scale(x)
    cos = cos_ref[...]
    sin = sin_ref[...]

    both = _dot(xn.astype(_BF16), wcat_s[...])
    a = both[:, :n_kv]
    c_kv = a[:, :KV_LORA]
    kr_ref[...] = _rotate(a[:, KV_LORA:], cos, sin).astype(_BF16)
    c_kv = (c_kv * _rms_scale(c_kv) * gkvlat_ref[...]).astype(_BF16)
    kn_ref[...] = _dot(c_kv, wuk_ref[...]).astype(_BF16)
    store_slabs(vt_ref, slice(None), _dot_nt(wuvt_ref[...], c_kv))

    proj = both[:, n_kv:]
    half_gate = 0.5 * proj[:, Q_LORA:]
    sg_ref[...] = (half_gate * jnp.tanh(half_gate) + half_gate).astype(_BF16)
    c_q = proj[:, :Q_LORA]
    c_q = (c_q * _rms_scale(c_q) * gqlat_ref[...]).astype(_BF16)
    qt = _dot_nt(wuqt_ref[...], c_q)
    scale = (MLA_NOPE + MLA_ROPE) ** -0.5 * LOG2_E
    half = MLA_ROPE // 2
    cos_rows = cost_ref[...]
    sin_rows = sint_ref[...]
    zero_rows = jnp.zeros((LANES // 2 - half, qt.shape[1]), _BF16)
    for h in range(MLA_HEADS):
        src = h * (MLA_NOPE + MLA_ROPE)
        dst = h * MLA_QK_PAD
        store_slabs(qt_ref, slice(dst, dst + MLA_NOPE), qt[src:src + MLA_NOPE] * scale)
        x1 = qt[src + MLA_NOPE:src + MLA_NOPE + half]
        x2 = qt[src + MLA_NOPE + half:src + MLA_NOPE + 2 * half]
        dst += MLA_NOPE
        store_slabs(qt_ref, slice(dst, dst + half), (x1 * cos_rows - x2 * sin_rows) * scale)
        store_slabs(qt_ref, slice(dst + half, dst + LANES // 2), zero_rows)
        dst += LANES // 2
        store_slabs(qt_ref, slice(dst, dst + half), (x1 * sin_rows + x2 * cos_rows) * scale)
        store_slabs(qt_ref, slice(dst + half, dst + LANES // 2), zero_rows)


def _spread_rope_columns(w):
    half = MLA_ROPE // 2
    z = jnp.zeros(w.shape[:-1] + (LANES // 2 - half,), w.dtype)
    return jnp.concatenate([w[..., :half], z, w[..., half:], z], axis=-1)


def _projections(h, g_kv, w_kv_a, g_kv_lat, w_uk, w_uv, g_pre, w_in, g_q_lat, w_uq):
    b, s, d = h.shape
    t = b * s
    tb = PROJ_TOKEN_BLOCK
    half = MLA_ROPE // 2
    cos, sin = _rope_angles(s, half)
    ones = np.ones((s, LANES // 2 - half), np.float32)
    zeros = np.zeros((s, LANES // 2 - half), np.float32)
    cos_t = np.concatenate([cos, ones, cos, ones], axis=-1)
    sin_t = np.concatenate([-sin, zeros, sin, zeros], axis=-1)

    wkva = jnp.concatenate([w_kv_a[:, :KV_LORA], _spread_rope_columns(w_kv_a[:, KV_LORA:])], axis=-1)

    const = lambda i: (0, 0)
    tok = lambda i: (i, 0)
    pos = lambda i: (i % (s // tb), 0)
    pos_t = lambda i: (0, i % (s // tb))
    weights = [
        g_kv.reshape(d, 1), wkva, g_kv_lat.reshape(1, KV_LORA), w_uk.astype(_BF16),
        w_uv.T.astype(_BF16), g_pre.reshape(d, 1), w_in, g_q_lat.reshape(1, Q_LORA),
        w_uq.T.astype(_BF16),
    ]

    def token_major(width):
        return jax.ShapeDtypeStruct((t, width), _BF16), pl.BlockSpec((tb, width), tok)

    def feature_major(width):
        return (jax.ShapeDtypeStruct((t // ATTN_BLOCK, width, ATTN_BLOCK), _BF16),
                pl.BlockSpec((tb // ATTN_BLOCK, width, ATTN_BLOCK), lambda i: (i, 0, 0)))

    outs = [feature_major(MLA_HEADS * MLA_QK_PAD), token_major(MLA_HEADS * MLA_NOPE),
            token_major(LANES), feature_major(MLA_WIDTH), token_major(MLA_WIDTH)]
    qt, kn, kr, vt, sg = pl.pallas_call(
        _projection_kernel,
        out_shape=[o[0] for o in outs],
        grid=(t // tb,),
        in_specs=[pl.BlockSpec((tb, d), tok)]
        + [_single_buffered(w.shape, const) for w in weights]
        + [pl.BlockSpec((tb, LANES), pos), pl.BlockSpec((tb, LANES), pos),
           pl.BlockSpec((half, tb), pos_t), pl.BlockSpec((half, tb), pos_t)],
        out_specs=[o[1] for o in outs],
        scratch_shapes=[pltpu.VMEM((d, wkva.shape[1] + w_in.shape[1]), _BF16)],
        compiler_params=pltpu.CompilerParams(
            dimension_semantics=("arbitrary",),
            vmem_limit_bytes=VMEM_LIMIT_BYTES),
        name="latent_projections",
    )(h.reshape(t, d), *weights, cos_t, sin_t,
      np.ascontiguousarray(cos.T), np.ascontiguousarray(sin.T))
    kn, kr, sg = [o.reshape(b, s, -1) for o in (kn, kr, sg)]
    return (qt.reshape(b, s // ATTN_BLOCK, MLA_HEADS * MLA_QK_PAD, ATTN_BLOCK), kn, kr,
            vt.reshape(b, s // ATTN_BLOCK, MLA_WIDTH, ATTN_BLOCK), sg)


def _attention_kernel(qt_ref, kn_ref, kr_ref, vt_ref, sg_ref, h_ref, wout_ref, gpost_ref, o_ref,
                      m_s, den_s, acc_s, y_s):
    blk = ATTN_BLOCK
    qi = pl.program_id(1)
    key_chunk = lax.broadcasted_iota(jnp.int32, (blk, blk), 0) // CHUNK
    query_chunk = lax.broadcasted_iota(jnp.int32, (blk, blk), 1) // CHUNK
    visible = key_chunk <= query_chunk
    ones = jnp.ones((ONES_ROWS, blk), _BF16)

    heads = range(MLA_HEADS)

    def head_scores(ki):
        rows = pl.ds(pl.multiple_of(ki * blk, blk), blk)
        k_rope = kr_ref[0, rows, :]

        def score(h):
            qt = qt_ref[0, 0, h * MLA_QK_PAD:(h + 1) * MLA_QK_PAD, :]
            k = jnp.concatenate([kn_ref[0, rows, h * MLA_NOPE:(h + 1) * MLA_NOPE], k_rope], axis=-1)
            return _dot(k, qt)

        scores = [score(h) for h in heads[:SCORE_LOOKAHEAD]]
        for h in heads:
            if h + SCORE_LOOKAHEAD < MLA_HEADS:
                scores.append(score(h + SCORE_LOOKAHEAD))
            yield h, scores[h]

    def values(ki, h):
        return jnp.concatenate([vt_ref[0, ki, h * MLA_V:(h + 1) * MLA_V, :], ones], axis=0)

    def running_max_step(ki, carry=None):
        seen = jnp.logical_or(visible, ki < qi)
        for h, st in head_scores(ki):
            st = jnp.where(seen, st, -jnp.inf)
            m_prev = m_s[h]
            m_new = jnp.maximum(m_prev, jnp.max(st, axis=0, keepdims=True))
            pt = jnp.exp2(st - m_new).astype(_BF16)
            alpha = jnp.exp2(m_prev - m_new)
            acc_s[h] = alpha * acc_s[h] + _dot(values(ki, h), pt)
            m_s[h] = m_new
        return carry

    def fixed_reference_first_step(last):
        for h, st in head_scores(0):
            if last:
                st = jnp.where(visible, st, -jnp.inf)
            m_ref = jnp.max(st[0:CHUNK], axis=0, keepdims=True)
            m_s[h] = m_ref
            pt = jnp.exp2(st - m_ref).astype(_BF16)
            acc = _dot(values(0, h), pt)
            if last:
                finish_head(h, acc)
            else:
                acc_s[h] = acc

    def finish_head(h, acc):
        head = slice(h * MLA_V, (h + 1) * MLA_V)
        den_s[h] = acc[MLA_V:MLA_V + 1, :]
        o = (acc[0:MLA_V, :] / acc[MLA_V:MLA_V + 1, :]).T
        y_s[:, head] = (sg_ref[0, :, head].astype(_F32) * o).astype(_BF16)

    def finish():
        for h in heads:
            finish_head(h, acc_s[h])

    def fixed_reference_step(ki, last):
        for h, st in head_scores(ki):
            if last:
                st = jnp.where(visible, st, -jnp.inf)
            pt = jnp.exp2(st - m_s[h]).astype(_BF16)
            acc = acc_s[h] + _dot(values(ki, h), pt)
            if last:
                finish_head(h, acc)
            else:
                acc_s[h] = acc

    def reset():
        m_s[...] = jnp.full_like(m_s, -jnp.inf)
        acc_s[...] = jnp.zeros_like(acc_s)

    @pl.when(qi == 0)
    def _():
        fixed_reference_first_step(last=True)

    @pl.when(qi > 0)
    def _():
        fixed_reference_first_step(last=False)
        lax.fori_loop(1, qi, lambda ki, c: (fixed_reference_step(ki, last=False), c)[1], 0)
        fixed_reference_step(qi, last=True)

    in_range = jnp.max(den_s[...]) <= MAX_DENOMINATOR

    def write_output():
        y = _dot(y_s[...], wout_ref[...])
        o_ref[0] = h_ref[0] + y * _rms_scale(y) * gpost_ref[...]

    write_output()

    @pl.when(jnp.logical_not(in_range))
    def _():
        reset()
        lax.fori_loop(0, qi + 1, running_max_step, 0)
        finish()
        write_output()


def _attention_layer(h, qt, kn, kr, vt, sg, w_out, g_post):
    b, s, d = h.shape
    blk = ATTN_BLOCK
    assert vt.shape == (b, s // blk, MLA_WIDTH, blk)
    assert qt.shape == (b, s // blk, MLA_HEADS * MLA_QK_PAD, blk)
    whole = lambda i, j: (i, 0, 0)
    qblk = lambda i, j: (i, j, 0)
    const = lambda i, j: (0, 0)
    return pl.pallas_call(
        _attention_kernel,
        out_shape=jax.ShapeDtypeStruct((b, s, d), _F32),
        grid=(b, s // blk),
        in_specs=[
            pl.BlockSpec((1, 1) + qt.shape[2:], lambda i, j: (i, j, 0, 0)),
            pl.BlockSpec((1, s, kn.shape[-1]), whole),
            pl.BlockSpec((1, s, kr.shape[-1]), whole),
            pl.BlockSpec((1,) + vt.shape[1:], lambda i, j: (i, 0, 0, 0)),
            pl.BlockSpec((1, blk, MLA_WIDTH), qblk),
            pl.BlockSpec((1, blk, d), qblk),
            _single_buffered(w_out.shape, const),
            _single_buffered((1, d), const),
        ],
        out_specs=pl.BlockSpec((1, blk, d), qblk),
        scratch_shapes=[
            pltpu.VMEM((MLA_HEADS, 1, blk), _F32),
            pltpu.VMEM((MLA_HEADS, 1, blk), _F32),
            pltpu.VMEM((MLA_HEADS, MLA_V + ONES_ROWS, blk), _F32),
            pltpu.VMEM((blk, MLA_WIDTH), _BF16),
        ],
        compiler_params=pltpu.CompilerParams(
            dimension_semantics=("parallel", "arbitrary"),
            vmem_limit_bytes=VMEM_LIMIT_BYTES),
        name="latent_attention",
    )(qt, kn, kr, vt, sg, h, w_out.astype(_BF16), g_post.reshape(1, d))


@jax.jit
def kernel(x, g_pre_a, w_in_a, gn_gain_a, w_out_a, g_post_a, g_kv, w_kv_a, g_kv_lat, w_uk, w_uv,
           g_pre_b, w_in_b, g_q_lat, w_uq, w_out_b, g_post_b):
    assert w_in_a.shape[0] == 1 and w_in_b.shape[0] == 1
    h = _retention_layer(x, g_pre_a[0], w_in_a[0], gn_gain_a[0], w_out_a[0], g_post_a[0])
    qt, kn, kr, vt, sg = _projections(h, g_kv, w_kv_a, g_kv_lat, w_uk, w_uv,
                                      g_pre_b[0], w_in_b[0], g_q_lat[0], w_uq[0])
    return _attention_layer(h, qt, kn, kr, vt, sg, w_out_b[0], g_post_b[0])
```

```python
import jax
import jax.numpy as jnp
import numpy as np
from jax import lax
from jax.experimental import pallas as pl
from jax.experimental.pallas import tpu as pltpu

D_MODEL = 1024
CHUNK = 64
RMS_EPS = 1e-6
ROPE_BASE = 10000.0

RET_HEADS = 8
RET_QK = 128
RET_V = 256
RET_QK_WIDTH = RET_HEADS * RET_QK
RET_WIDTH = RET_HEADS * RET_V

MLA_HEADS = 8
MLA_NOPE = 128
MLA_ROPE = 64
MLA_V = 128
MLA_WIDTH = MLA_HEADS * MLA_V
Q_LORA = 384
KV_LORA = 256

LANES = 128
MLA_QK_PAD = 2 * LANES
VMEM_BYTES_V7X = 64 * 1024 * 1024
VMEM_LIMIT_BYTES = VMEM_BYTES_V7X - 4 * 1024 * 1024

RET_TOKEN_BLOCK = 512
WEIGHT_CHUNK_ROWS = 1024
WEIGHT_CHUNK_COLS = 512
RET_SPAN = 256
PROJ_TOKEN_BLOCK = 1024
ATTN_BLOCK = 512
SCORE_LOOKAHEAD = 2
ONES_ROWS = 16
MAX_DENOMINATOR = 2.0 ** 24

LOG2_E = 1.4426950408889634

_BF16 = jnp.bfloat16
_F32 = jnp.float32


def _single_buffered(shape, index_map):
    return pl.BlockSpec(shape, index_map, pipeline_mode=pl.Buffered(1))


def _rms_scale(x):
    return lax.rsqrt(jnp.mean(x * x, axis=-1, keepdims=True) + RMS_EPS)


def _rotate(x, cos, sin_signed):
    return x * cos + pltpu.roll(x, LANES // 2, axis=1) * sin_signed


def _dot(a, b):
    return jnp.dot(a, b, preferred_element_type=_F32)


def _dot_nt(a, b):
    return lax.dot_general(a, b, (((1,), (1,)), ((), ())), preferred_element_type=_F32)


def _dot_tn(a, b):
    return lax.dot_general(a, b, (((0,), (0,)), ((), ())), preferred_element_type=_F32)


def _weight_chunks(win_hbm, wout_hbm, win_ref, wout_ref):
    rows, cols = WEIGHT_CHUNK_ROWS, WEIGHT_CHUNK_COLS
    pairs = []
    for src, dst in ((win_hbm, win_ref), (wout_hbm, wout_ref)):
        for r in range(0, src.shape[0], rows):
            for c in range(0, src.shape[1], cols):
                pairs.append((src.at[pl.ds(r, rows), pl.ds(c, cols)], dst.at[pl.ds(r, rows), pl.ds(c, cols)]))
    return pairs


def _retention_kernel(x_ref, gpre_ref, win_hbm, cos_ref, sin_ref, dmask_ref, xi_ref, zeta_ref,
                      gspan_ref, gn_ref, wout_hbm, gpost_ref, o_ref,
                      state_ref, q_s, k_s, v_s, gate_s, y_s, win_ref, wout_ref, stage, stage_sem):
    @pl.when(jnp.logical_and(pl.program_id(0) == 0, pl.program_id(1) == 0))
    def _():
        pairs = _weight_chunks(win_hbm, wout_hbm, win_ref, wout_ref)

        def copy(n):
            return pltpu.make_async_copy(pairs[n][0], stage.at[n % 2], stage_sem.at[n % 2])

        copy(0).start()
        for n in range(len(pairs)):
            if n + 1 < len(pairs):
                copy(n + 1).start()
            copy(n).wait()
            pairs[n][1][...] = stage[n % 2].astype(_BF16)

    @pl.when(pl.program_id(1) == 0)
    def _():
        state_ref[...] = jnp.zeros_like(state_ref)

    x = x_ref[0]
    hb = (x * _rms_scale(x) * gpre_ref[...]).astype(_BF16)
    cos = cos_ref[...]
    sin = sin_ref[...]

    q_all = _dot(hb, win_ref[:, 0:RET_QK_WIDTH])
    for h in range(RET_HEADS):
        q_s[h] = _rotate(q_all[:, h * RET_QK:(h + 1) * RET_QK], cos, sin)
    k_all = _dot(hb, win_ref[:, RET_QK_WIDTH:2 * RET_QK_WIDTH])
    for h in range(RET_HEADS):
        k_s[h] = _rotate(k_all[:, h * RET_QK:(h + 1) * RET_QK], cos, sin)
    v_s[...] = _dot(hb, win_ref[:, 2 * RET_QK_WIDTH:2 * RET_QK_WIDTH + RET_WIDTH]).astype(_BF16)
    gate_s[...] = _dot(hb, win_ref[:, 2 * RET_QK_WIDTH + RET_WIDTH:])

    for s in range(RET_TOKEN_BLOCK // RET_SPAN):
        rows = slice(s * RET_SPAN, (s + 1) * RET_SPAN)
        for h in range(RET_HEADS):
            cols = slice(h * RET_V, (h + 1) * RET_V)
            q = q_s[h, rows, :]
            k = k_s[h, rows, :]
            v = v_s[rows, cols]
            scores = _dot_nt(q.astype(_BF16), k.astype(_BF16)) * dmask_ref[h]
            state = state_ref[h]
            o = _dot(scores.astype(_BF16), v)
            o = o + _dot((q * xi_ref[h]).astype(_BF16), state.astype(_BF16))
            state_ref[h] = state * gspan_ref[h] + _dot_tn((k * zeta_ref[h]).astype(_BF16), v)
            mu = jnp.mean(o, axis=-1, keepdims=True)
            oc = o - mu
            var = jnp.mean(oc * oc, axis=-1, keepdims=True)
            on = oc * lax.rsqrt(var + RMS_EPS) * gn_ref[:, cols]
            half_gate = 0.5 * gate_s[rows, cols]
            silu = half_gate * jnp.tanh(half_gate) + half_gate
            y_s[rows, cols] = (silu * on).astype(_BF16)

    y = _dot(y_s[...], wout_ref[...])
    o_ref[0] = x + y * _rms_scale(y) * gpost_ref[...]


def _retention_tables():
    log_g = np.log(1.0 - np.exp2(-5.0 - np.arange(RET_HEADS, dtype=np.float64)))
    idx = np.arange(RET_SPAN, dtype=np.float64)
    diff = idx[:, None] - idx[None, :]
    chunk = np.arange(RET_SPAN) // CHUNK
    same = chunk[:, None] == chunk[None, :]
    earlier = chunk[None, :] < chunk[:, None]
    dist = np.where(same, np.abs(diff), diff)
    decay = np.exp(log_g[:, None, None] * np.where(same | earlier, dist, 0.0)[None])
    k_scale = RET_QK ** -0.5
    dmask = np.where((same | earlier)[None], decay, 0.0) * k_scale
    xi = np.exp(log_g[:, None] * (idx + 1.0)[None]) * k_scale
    zeta = np.exp(log_g[:, None] * (RET_SPAN - 1.0 - idx)[None])
    g_span = np.exp(log_g * RET_SPAN)
    xi = np.broadcast_to(xi[:, :, None], (RET_HEADS, RET_SPAN, RET_QK))
    zeta = np.broadcast_to(zeta[:, :, None], (RET_HEADS, RET_SPAN, RET_QK))
    g_span = np.broadcast_to(g_span[:, None, None], (RET_HEADS, 1, RET_V))
    return tuple(np.ascontiguousarray(t, dtype=np.float32) for t in (dmask, xi, zeta, g_span))


def _rope_angles(seq, half):
    inv = ROPE_BASE ** (-np.arange(half, dtype=np.float64) / half)
    ang = np.arange(seq, dtype=np.float64)[:, None] * inv[None, :]
    return np.cos(ang).astype(np.float32), np.sin(ang).astype(np.float32)


def _retention_layer(x, g_pre, w_in, gn_gain, w_out, g_post):
    b, s, d = x.shape
    tb = RET_TOKEN_BLOCK
    cos, sin = _rope_angles(s, RET_QK // 2)
    cos_t = np.concatenate([cos, cos], axis=-1)
    sin_t = np.concatenate([-sin, sin], axis=-1)
    dmask, xi, zeta, g_span = _retention_tables()
    const2 = lambda i, j: (0, 0)
    const3 = lambda i, j: (0, 0, 0)
    return pl.pallas_call(
        _retention_kernel,
        out_shape=jax.ShapeDtypeStruct((b, s, d), _F32),
        grid=(b, s // tb),
        in_specs=[
            pl.BlockSpec((1, tb, d), lambda i, j: (i, j, 0)),
            _single_buffered((1, d), const2),
            pl.BlockSpec(memory_space=pl.ANY),
            pl.BlockSpec((tb, RET_QK), lambda i, j: (j, 0)),
            pl.BlockSpec((tb, RET_QK), lambda i, j: (j, 0)),
            _single_buffered(dmask.shape, const3),
            _single_buffered(xi.shape, const3),
            _single_buffered(zeta.shape, const3),
            _single_buffered(g_span.shape, const3),
            _single_buffered((1, RET_WIDTH), const2),
            pl.BlockSpec(memory_space=pl.ANY),
            _single_buffered((1, d), const2),
        ],
        out_specs=pl.BlockSpec((1, tb, d), lambda i, j: (i, j, 0)),
        scratch_shapes=[
            pltpu.VMEM((RET_HEADS, RET_QK, RET_V), _F32),
            pltpu.VMEM((RET_HEADS, tb, RET_QK), _F32),
            pltpu.VMEM((RET_HEADS, tb, RET_QK), _F32),
            pltpu.VMEM((tb, RET_WIDTH), _BF16),
            pltpu.VMEM((tb, RET_WIDTH), _F32),
            pltpu.VMEM((tb, RET_WIDTH), _BF16),
            pltpu.VMEM(w_in.shape, _BF16),
            pltpu.VMEM(w_out.shape, _BF16),
            pltpu.VMEM((2, WEIGHT_CHUNK_ROWS, WEIGHT_CHUNK_COLS), _F32),
            pltpu.SemaphoreType.DMA((2,)),
        ],
        compiler_params=pltpu.CompilerParams(
            dimension_semantics=("arbitrary", "arbitrary"),
            vmem_limit_bytes=VMEM_LIMIT_BYTES),
        name="retention_layer",
    )(x, g_pre.reshape(1, d), w_in, cos_t, sin_t, dmask, xi, zeta, g_span,
      gn_gain.reshape(1, RET_WIDTH), w_out, g_post.reshape(1, d))


def _projection_kernel(h_ref, gkv_ref, wkva_ref, gkvlat_ref, wuk_ref, wuvt_ref, gpre_ref, winb_ref,
                       gqlat_ref, wuqt_ref, cos_ref, sin_ref, cost_ref, sint_ref,
                       qt_ref, kn_ref, kr_ref, vt_ref, sg_ref, wcat_s):
    def store_slabs(ref, rows, value):
        for i in range(value.shape[1] // ATTN_BLOCK):
            ref[i, rows, :] = value[:, i * ATTN_BLOCK:(i + 1) * ATTN_BLOCK].astype(_BF16)

    n_kv = wkva_ref.shape[1]

    @pl.when(pl.program_id(0) == 0)
    def _():
        wcat_s[:, :n_kv] = (wkva_ref[...] * gkv_ref[...]).astype(_BF16)
        wcat_s[:, n_kv:] = (winb_ref[...] * gpre_ref[...]).astype(_BF16)

    x = h_ref[...]
    xn = x * _rms_scale(x)
    cos = cos_ref[...]
    sin = sin_ref[...]

    both = _dot(xn.astype(_BF16), wcat_s[...])
    a = both[:, :n_kv]
    c_kv = a[:, :KV_LORA]
    kr_ref[...] = _rotate(a[:, KV_LORA:], cos, sin).astype(_BF16)
    c_kv = (c_kv * _rms_scale(c_kv) * gkvlat_ref[...]).astype(_BF16)
    kn_ref[...] = _dot(c_kv, wuk_ref[...]).astype(_BF16)
    store_slabs(vt_ref, slice(None), _dot_nt(wuvt_ref[...], c_kv))

    proj = both[:, n_kv:]
    half_gate = 0.5 * proj[:, Q_LORA:]
    sg_ref[...] = (half_gate * jnp.tanh(half_gate) + half_gate).astype(_BF16)
    c_q = proj[:, :Q_LORA]
    c_q = (c_q * _rms_scale(c_q) * gqlat_ref[...]).astype(_BF16)
    qt = _dot_nt(wuqt_ref[...], c_q)
    scale = (MLA_NOPE + MLA_ROPE) ** -0.5 * LOG2_E
    half = MLA_ROPE // 2
    cos_rows = cost_ref[...]
    sin_rows = sint_ref[...]
    zero_rows = jnp.zeros((LANES // 2 - half, qt.shape[1]), _BF16)
    for h in range(MLA_HEADS):
        src = h * (MLA_NOPE + MLA_ROPE)
        dst = h * MLA_QK_PAD
        store_slabs(qt_ref, slice(dst, dst + MLA_NOPE), qt[src:src + MLA_NOPE] * scale)
        x1 = qt[src + MLA_NOPE:src + MLA_NOPE + half]
        x2 = qt[src + MLA_NOPE + half:src + MLA_NOPE + 2 * half]
        dst += MLA_NOPE
        store_slabs(qt_ref, slice(dst, dst + half), (x1 * cos_rows - x2 * sin_rows) * scale)
        store_slabs(qt_ref, slice(dst + half, dst + LANES // 2), zero_rows)
        dst += LANES // 2
        store_slabs(qt_ref, slice(dst, dst + half), (x1 * sin_rows + x2 * cos_rows) * scale)
        store_slabs(qt_ref, slice(dst + half, dst + LANES // 2), zero_rows)


def _spread_rope_columns(w):
    half = MLA_ROPE // 2
    z = jnp.zeros(w.shape[:-1] + (LANES // 2 - half,), w.dtype)
    return jnp.concatenate([w[..., :half], z, w[..., half:], z], axis=-1)


def _projections(h, g_kv, w_kv_a, g_kv_lat, w_uk, w_uv, g_pre, w_in, g_q_lat, w_uq):
    b, s, d = h.shape
    t = b * s
    tb = PROJ_TOKEN_BLOCK
    half = MLA_ROPE // 2
    cos, sin = _rope_angles(s, half)
    ones = np.ones((s, LANES // 2 - half), np.float32)
    zeros = np.zeros((s, LANES // 2 - half), np.float32)
    cos_t = np.concatenate([cos, ones, cos, ones], axis=-1)
    sin_t = np.concatenate([-sin, zeros, sin, zeros], axis=-1)

    wkva = jnp.concatenate([w_kv_a[:, :KV_LORA], _spread_rope_columns(w_kv_a[:, KV_LORA:])], axis=-1)

    const = lambda i: (0, 0)
    tok = lambda i: (i, 0)
    pos = lambda i: (i % (s // tb), 0)
    pos_t = lambda i: (0, i % (s // tb))
    weights = [
        g_kv.reshape(d, 1), wkva, g_kv_lat.reshape(1, KV_LORA), w_uk.astype(_BF16),
        w_uv.T.astype(_BF16), g_pre.reshape(d, 1), w_in, g_q_lat.reshape(1, Q_LORA),
        w_uq.T.astype(_BF16),
    ]

    def token_major(width):
        return jax.ShapeDtypeStruct((t, width), _BF16), pl.BlockSpec((tb, width), tok)

    def feature_major(width):
        return (jax.ShapeDtypeStruct((t // ATTN_BLOCK, width, ATTN_BLOCK), _BF16),
                pl.BlockSpec((tb // ATTN_BLOCK, width, ATTN_BLOCK), lambda i: (i, 0, 0)))

    outs = [feature_major(MLA_HEADS * MLA_QK_PAD), token_major(MLA_HEADS * MLA_NOPE),
            token_major(LANES), feature_major(MLA_WIDTH), token_major(MLA_WIDTH)]
    qt, kn, kr, vt, sg = pl.pallas_call(
        _projection_kernel,
        out_shape=[o[0] for o in outs],
        grid=(t // tb,),
        in_specs=[pl.BlockSpec((tb, d), tok)]
        + [_single_buffered(w.shape, const) for w in weights]
        + [pl.BlockSpec((tb, LANES), pos), pl.BlockSpec((tb, LANES), pos),
           pl.BlockSpec((half, tb), pos_t), pl.BlockSpec((half, tb), pos_t)],
        out_specs=[o[1] for o in outs],
        scratch_shapes=[pltpu.VMEM((d, wkva.shape[1] + w_in.shape[1]), _BF16)],
        compiler_params=pltpu.CompilerParams(
            dimension_semantics=("arbitrary",),
            vmem_limit_bytes=VMEM_LIMIT_BYTES),
        name="latent_projections",
    )(h.reshape(t, d), *weights, cos_t, sin_t,
      np.ascontiguousarray(cos.T), np.ascontiguousarray(sin.T))
    kn, kr, sg = [o.reshape(b, s, -1) for o in (kn, kr, sg)]
    return (qt.reshape(b, s // ATTN_BLOCK, MLA_HEADS * MLA_QK_PAD, ATTN_BLOCK), kn, kr,
            vt.reshape(b, s // ATTN_BLOCK, MLA_WIDTH, ATTN_BLOCK), sg)


def _attention_kernel(qt_ref, kn_ref, kr_ref, vt_ref, sg_ref, h_ref, wout_ref, gpost_ref, o_ref,
                      m_s, den_s, acc_s, y_s):
    blk = ATTN_BLOCK
    qi = pl.program_id(1)
    key_chunk = lax.broadcasted_iota(jnp.int32, (blk, blk), 0) // CHUNK
    query_chunk = lax.broadcasted_iota(jnp.int32, (blk, blk), 1) // CHUNK
    visible = key_chunk <= query_chunk
    ones = jnp.ones((ONES_ROWS, blk), _BF16)

    heads = range(MLA_HEADS)

    def head_scores(ki):
        rows = pl.ds(pl.multiple_of(ki * blk, blk), blk)
        k_rope = kr_ref[0, rows, :]

        def score(h):
            qt = qt_ref[0, 0, h * MLA_QK_PAD:(h + 1) * MLA_QK_PAD, :]
            k = jnp.concatenate([kn_ref[0, rows, h * MLA_NOPE:(h + 1) * MLA_NOPE], k_rope], axis=-1)
            return _dot(k, qt)

        scores = [score(h) for h in heads[:SCORE_LOOKAHEAD]]
        for h in heads:
            if h + SCORE_LOOKAHEAD < MLA_HEADS:
                scores.append(score(h + SCORE_LOOKAHEAD))
            yield h, scores[h]

    def values(ki, h):
        return jnp.concatenate([vt_ref[0, ki, h * MLA_V:(h + 1) * MLA_V, :], ones], axis=0)

    def running_max_step(ki, carry=None):
        seen = jnp.logical_or(visible, ki < qi)
        for h, st in head_scores(ki):
            st = jnp.where(seen, st, -jnp.inf)
            m_prev = m_s[h]
            m_new = jnp.maximum(m_prev, jnp.max(st, axis=0, keepdims=True))
            pt = jnp.exp2(st - m_new).astype(_BF16)
            alpha = jnp.exp2(m_prev - m_new)
            acc_s[h] = alpha * acc_s[h] + _dot(values(ki, h), pt)
            m_s[h] = m_new
        return carry

    def fixed_reference_first_step(last):
        for h, st in head_scores(0):
            if last:
                st = jnp.where(visible, st, -jnp.inf)
            m_ref = jnp.max(st[0:CHUNK], axis=0, keepdims=True)
            m_s[h] = m_ref
            pt = jnp.exp2(st - m_ref).astype(_BF16)
            acc = _dot(values(0, h), pt)
            if last:
                finish_head(h, acc)
            else:
                acc_s[h] = acc

    def finish_head(h, acc):
        head = slice(h * MLA_V, (h + 1) * MLA_V)
        den_s[h] = acc[MLA_V:MLA_V + 1, :]
        o = (acc[0:MLA_V, :] / acc[MLA_V:MLA_V + 1, :]).T
        y_s[:, head] = (sg_ref[0, :, head].astype(_F32) * o).astype(_BF16)

    def finish():
        for h in heads:
            finish_head(h, acc_s[h])

    def fixed_reference_step(ki, last):
        for h, st in head_scores(ki):
            if last:
                st = jnp.where(visible, st, -jnp.inf)
            pt = jnp.exp2(st - m_s[h]).astype(_BF16)
            acc = acc_s[h] + _dot(values(ki, h), pt)
            if last:
                finish_head(h, acc)
            else:
                acc_s[h] = acc

    def reset():
        m_s[...] = jnp.full_like(m_s, -jnp.inf)
        acc_s[...] = jnp.zeros_like(acc_s)

    @pl.when(qi == 0)
    def _():
        fixed_reference_first_step(last=True)

    @pl.when(qi > 0)
    def _():
        fixed_reference_first_step(last=False)
        lax.fori_loop(1, qi, lambda ki, c: (fixed_reference_step(ki, last=False), c)[1], 0)
        fixed_reference_step(qi, last=True)

    in_range = jnp.max(den_s[...]) <= MAX_DENOMINATOR

    def write_output():
        y = _dot(y_s[...], wout_ref[...])
        o_ref[0] = h_ref[0] + y * _rms_scale(y) * gpost_ref[...]

    write_output()

    @pl.when(jnp.logical_not(in_range))
    def _():
        reset()
        lax.fori_loop(0, qi + 1, running_max_step, 0)
        finish()
        write_output()


def _attention_layer(h, qt, kn, kr, vt, sg, w_out, g_post):
    b, s, d = h.shape
    blk = ATTN_BLOCK
    assert vt.shape == (b, s // blk, MLA_WIDTH, blk)
    assert qt.shape == (b, s // blk, MLA_HEADS * MLA_QK_PAD, blk)
    whole = lambda i, j: (i, 0, 0)
    qblk = lambda i, j: (i, j, 0)
    const = lambda i, j: (0, 0)
    return pl.pallas_call(
        _attention_kernel,
        out_shape=jax.ShapeDtypeStruct((b, s, d), _F32),
        grid=(b, s // blk),
        in_specs=[
            pl.BlockSpec((1, 1) + qt.shape[2:], lambda i, j: (i, j, 0, 0)),
            pl.BlockSpec((1, s, kn.shape[-1]), whole),
            pl.BlockSpec((1, s, kr.shape[-1]), whole),
            pl.BlockSpec((1,) + vt.shape[1:], lambda i, j: (i, 0, 0, 0)),
            pl.BlockSpec((1, blk, MLA_WIDTH), qblk),
            pl.BlockSpec((1, blk, d), qblk),
            _single_buffered(w_out.shape, const),
            _single_buffered((1, d), const),
        ],
        out_specs=pl.BlockSpec((1, blk, d), qblk),
        scratch_shapes=[
            pltpu.VMEM((MLA_HEADS, 1, blk), _F32),
            pltpu.VMEM((MLA_HEADS, 1, blk), _F32),
            pltpu.VMEM((MLA_HEADS, MLA_V + ONES_ROWS, blk), _F32),
            pltpu.VMEM((blk, MLA_WIDTH), _BF16),
        ],
        compiler_params=pltpu.CompilerParams(
            dimension_semantics=("parallel", "arbitrary"),
            vmem_limit_bytes=VMEM_LIMIT_BYTES),
        name="latent_attention",
    )(qt, kn, kr, vt, sg, h, w_out.astype(_BF16), g_post.reshape(1, d))


@jax.jit
def kernel(x, g_pre_a, w_in_a, gn_gain_a, w_out_a, g_post_a, g_kv, w_kv_a, g_kv_lat, w_uk, w_uv,
           g_pre_b, w_in_b, g_q_lat, w_uq, w_out_b, g_post_b):
    assert w_in_a.shape[0] == 1 and w_in_b.shape[0] == 1
    h = _retention_layer(x, g_pre_a[0], w_in_a[0], gn_gain_a[0], w_out_a[0], g_post_a[0])
    qt, kn, kr, vt, sg = _projections(h, g_kv, w_kv_a, g_kv_lat, w_uk, w_uv,
                                      g_pre_b[0], w_in_b[0], g_q_lat[0], w_uq[0])
    return _attention_layer(h, qt, kn, kr, vt, sg, w_out_b[0], g_post_b[0])
```

```python
import jax
import jax.numpy as jnp
import numpy as np
from jax import lax
from jax.experimental import pallas as pl
from jax.experimental.pallas import tpu as pltpu

D_MODEL = 1024
CHUNK = 64
RMS_EPS = 1e-6
ROPE_BASE = 10000.0

RET_HEADS = 8
RET_QK = 128
RET_V = 256
RET_QK_WIDTH = RET_HEADS * RET_QK
RET_WIDTH = RET_HEADS * RET_V

MLA_HEADS = 8
MLA_NOPE = 128
MLA_ROPE = 64
MLA_V = 128
MLA_WIDTH = MLA_HEADS * MLA_V
Q_LORA = 384
KV_LORA = 256

LANES = 128
MLA_QK_PAD = 2 * LANES
VMEM_BYTES_V7X = 64 * 1024 * 1024
VMEM_LIMIT_BYTES = VMEM_BYTES_V7X - 4 * 1024 * 1024

RET_TOKEN_BLOCK = 512
WEIGHT_CHUNK_ROWS = 1024
WEIGHT_CHUNK_COLS = 512
WEIGHT_STAGE_SLOTS = 4
RET_SPAN = 256
PROJ_TOKEN_BLOCK = 1024
ATTN_BLOCK = 512
SCORE_LOOKAHEAD = 2
ONES_ROWS = 16
MAX_DENOMINATOR = 2.0 ** 24

LOG2_E = 1.4426950408889634

_BF16 = jnp.bfloat16
_F32 = jnp.float32


def _single_buffered(shape, index_map):
    return pl.BlockSpec(shape, index_map, pipeline_mode=pl.Buffered(1))


def _rms_scale(x):
    return lax.rsqrt(jnp.mean(x * x, axis=-1, keepdims=True) + RMS_EPS)


def _rotate(x, cos, sin_signed):
    return x * cos + pltpu.roll(x, LANES // 2, axis=1) * sin_signed


def _dot(a, b):
    return jnp.dot(a, b, preferred_element_type=_F32)


def _dot_nt(a, b):
    return lax.dot_general(a, b, (((1,), (1,)), ((), ())), preferred_element_type=_F32)


def _dot_tn(a, b):
    return lax.dot_general(a, b, (((0,), (0,)), ((), ())), preferred_element_type=_F32)


def _weight_chunks(win_hbm, wout_hbm, win_ref, wout_ref):
    rows, cols = WEIGHT_CHUNK_ROWS, WEIGHT_CHUNK_COLS
    pairs = []
    for src, dst in ((win_hbm, win_ref), (wout_hbm, wout_ref)):
        for r in range(0, src.shape[0], rows):
            for c in range(0, src.shape[1], cols):
                pairs.append((src.at[pl.ds(r, rows), pl.ds(c, cols)], dst.at[pl.ds(r, rows), pl.ds(c, cols)]))
    return pairs


def _retention_kernel(x_ref, gpre_ref, win_hbm, cos_ref, sin_ref, dmask_ref, xi_ref, zeta_ref,
                      gspan_ref, gn_ref, wout_hbm, gpost_ref, o_ref,
                      state_ref, q_s, k_s, v_s, gate_s, y_s, win_ref, wout_ref, stage, stage_sem):
    @pl.when(jnp.logical_and(pl.program_id(0) == 0, pl.program_id(1) == 0))
    def _():
        pairs = _weight_chunks(win_hbm, wout_hbm, win_ref, wout_ref)

        slots = stage.shape[0]

        def copy(n):
            return pltpu.make_async_copy(pairs[n][0], stage.at[n % slots], stage_sem.at[n % slots])

        for n in range(slots - 1):
            copy(n).start()
        for n in range(len(pairs)):
            if n + slots - 1 < len(pairs):
                copy(n + slots - 1).start()
            copy(n).wait()
            pairs[n][1][...] = stage[n % slots].astype(_BF16)

    @pl.when(pl.program_id(1) == 0)
    def _():
        state_ref[...] = jnp.zeros_like(state_ref)

    x = x_ref[0]
    hb = (x * _rms_scale(x) * gpre_ref[...]).astype(_BF16)
    cos = cos_ref[...]
    sin = sin_ref[...]

    q_all = _dot(hb, win_ref[:, 0:RET_QK_WIDTH])
    for h in range(RET_HEADS):
        q_s[h] = _rotate(q_all[:, h * RET_QK:(h + 1) * RET_QK], cos, sin)
    k_all = _dot(hb, win_ref[:, RET_QK_WIDTH:2 * RET_QK_WIDTH])
    for h in range(RET_HEADS):
        k_s[h] = _rotate(k_all[:, h * RET_QK:(h + 1) * RET_QK], cos, sin)
    v_s[...] = _dot(hb, win_ref[:, 2 * RET_QK_WIDTH:2 * RET_QK_WIDTH + RET_WIDTH]).astype(_BF16)
    gate_s[...] = _dot(hb, win_ref[:, 2 * RET_QK_WIDTH + RET_WIDTH:])

    for s in range(RET_TOKEN_BLOCK // RET_SPAN):
        rows = slice(s * RET_SPAN, (s + 1) * RET_SPAN)
        for h in range(RET_HEADS):
            cols = slice(h * RET_V, (h + 1) * RET_V)
            q = q_s[h, rows, :]
            k = k_s[h, rows, :]
            v = v_s[rows, cols]
            scores = _dot_nt(q.astype(_BF16), k.astype(_BF16)) * dmask_ref[h]
            state = state_ref[h]
            o = _dot(scores.astype(_BF16), v)
            o = o + _dot((q * xi_ref[h]).astype(_BF16), state.astype(_BF16))
            state_ref[h] = state * gspan_ref[h] + _dot_tn((k * zeta_ref[h]).astype(_BF16), v)
            mu = jnp.mean(o, axis=-1, keepdims=True)
            oc = o - mu
            var = jnp.mean(oc * oc, axis=-1, keepdims=True)
            on = oc * lax.rsqrt(var + RMS_EPS) * gn_ref[:, cols]
            half_gate = 0.5 * gate_s[rows, cols]
            silu = half_gate * jnp.tanh(half_gate) + half_gate
            y_s[rows, cols] = (silu * on).astype(_BF16)

    y = _dot(y_s[...], wout_ref[...])
    o_ref[0] = x + y * _rms_scale(y) * gpost_ref[...]


def _retention_tables():
    log_g = np.log(1.0 - np.exp2(-5.0 - np.arange(RET_HEADS, dtype=np.float64)))
    idx = np.arange(RET_SPAN, dtype=np.float64)
    diff = idx[:, None] - idx[None, :]
    chunk = np.arange(RET_SPAN) // CHUNK
    same = chunk[:, None] == chunk[None, :]
    earlier = chunk[None, :] < chunk[:, None]
    dist = np.where(same, np.abs(diff), diff)
    decay = np.exp(log_g[:, None, None] * np.where(same | earlier, dist, 0.0)[None])
    k_scale = RET_QK ** -0.5
    dmask = np.where((same | earlier)[None], decay, 0.0) * k_scale
    xi = np.exp(log_g[:, None] * (idx + 1.0)[None]) * k_scale
    zeta = np.exp(log_g[:, None] * (RET_SPAN - 1.0 - idx)[None])
    g_span = np.exp(log_g * RET_SPAN)
    xi = np.broadcast_to(xi[:, :, None], (RET_HEADS, RET_SPAN, RET_QK))
    zeta = np.broadcast_to(zeta[:, :, None], (RET_HEADS, RET_SPAN, RET_QK))
    g_span = np.broadcast_to(g_span[:, None, None], (RET_HEADS, 1, RET_V))
    return tuple(np.ascontiguousarray(t, dtype=np.float32) for t in (dmask, xi, zeta, g_span))


def _rope_angles(seq, half):
    inv = ROPE_BASE ** (-np.arange(half, dtype=np.float64) / half)
    ang = np.arange(seq, dtype=np.float64)[:, None] * inv[None, :]
    return np.cos(ang).astype(np.float32), np.sin(ang).astype(np.float32)


def _retention_layer(x, g_pre, w_in, gn_gain, w_out, g_post):
    b, s, d = x.shape
    tb = RET_TOKEN_BLOCK
    cos, sin = _rope_angles(s, RET_QK // 2)
    cos_t = np.concatenate([cos, cos], axis=-1)
    sin_t = np.concatenate([-sin, sin], axis=-1)
    dmask, xi, zeta, g_span = _retention_tables()
    const2 = lambda i, j: (0, 0)
    const3 = lambda i, j: (0, 0, 0)
    return pl.pallas_call(
        _retention_kernel,
        out_shape=jax.ShapeDtypeStruct((b, s, d), _F32),
        grid=(b, s // tb),
        in_specs=[
            pl.BlockSpec((1, tb, d), lambda i, j: (i, j, 0)),
            _single_buffered((1, d), const2),
            pl.BlockSpec(memory_space=pl.ANY),
            pl.BlockSpec((tb, RET_QK), lambda i, j: (j, 0)),
            pl.BlockSpec((tb, RET_QK), lambda i, j: (j, 0)),
            _single_buffered(dmask.shape, const3),
            _single_buffered(xi.shape, const3),
            _single_buffered(zeta.shape, const3),
            _single_buffered(g_span.shape, const3),
            _single_buffered((1, RET_WIDTH), const2),
            pl.BlockSpec(memory_space=pl.ANY),
            _single_buffered((1, d), const2),
        ],
        out_specs=pl.BlockSpec((1, tb, d), lambda i, j: (i, j, 0)),
        scratch_shapes=[
            pltpu.VMEM((RET_HEADS, RET_QK, RET_V), _F32),
            pltpu.VMEM((RET_HEADS, tb, RET_QK), _F32),
            pltpu.VMEM((RET_HEADS, tb, RET_QK), _F32),
            pltpu.VMEM((tb, RET_WIDTH), _BF16),
            pltpu.VMEM((tb, RET_WIDTH), _F32),
            pltpu.VMEM((tb, RET_WIDTH), _BF16),
            pltpu.VMEM(w_in.shape, _BF16),
            pltpu.VMEM(w_out.shape, _BF16),
            pltpu.VMEM((WEIGHT_STAGE_SLOTS, WEIGHT_CHUNK_ROWS, WEIGHT_CHUNK_COLS), _F32),
            pltpu.SemaphoreType.DMA((WEIGHT_STAGE_SLOTS,)),
        ],
        compiler_params=pltpu.CompilerParams(
            dimension_semantics=("arbitrary", "arbitrary"),
            vmem_limit_bytes=VMEM_LIMIT_BYTES),
        name="retention_layer",
    )(x, g_pre.reshape(1, d), w_in, cos_t, sin_t, dmask, xi, zeta, g_span,
      gn_gain.reshape(1, RET_WIDTH), w_out, g_post.reshape(1, d))


def _projection_kernel(h_ref, gkv_ref, wkva_ref, gkvlat_ref, wuk_ref, wuvt_ref, gpre_ref, winb_ref,
                       gqlat_ref, wuqt_ref, cos_ref, sin_ref, cost_ref, sint_ref,
                       qt_ref, kn_ref, kr_ref, vt_ref, sg_ref, wcat_s):
    def store_slabs(ref, rows, value):
        for i in range(value.shape[1] // ATTN_BLOCK):
            ref[i, rows, :] = value[:, i * ATTN_BLOCK:(i + 1) * ATTN_BLOCK].astype(_BF16)

    n_kv = wkva_ref.shape[1]

    @pl.when(pl.program_id(0) == 0)
    def _():
        wcat_s[:, :n_kv] = (wkva_ref[...] * gkv_ref[...]).astype(_BF16)
        wcat_s[:, n_kv:] = (winb_ref[...] * gpre_ref[...]).astype(_BF16)

    x = h_ref[...]
    xn = x * _rms_scale(x)
    cos = cos_ref[...]
    sin = sin_ref[...]

    both = _dot(xn.astype(_BF16), wcat_s[...])
    a = both[:, :n_kv]
    c_kv = a[:, :KV_LORA]
    kr_ref[...] = _rotate(a[:, KV_LORA:], cos, sin).astype(_BF16)
    c_kv = (c_kv * _rms_scale(c_kv) * gkvlat_ref[...]).astype(_BF16)
    kn_ref[...] = _dot(c_kv, wuk_ref[...]).astype(_BF16)
    store_slabs(vt_ref, slice(None), _dot_nt(wuvt_ref[...], c_kv))

    proj = both[:, n_kv:]
    half_gate = 0.5 * proj[:, Q_LORA:]
    sg_ref[...] = (half_gate * jnp.tanh(half_gate) + half_gate).astype(_BF16)
    c_q = proj[:, :Q_LORA]
    c_q = (c_q * _rms_scale(c_q) * gqlat_ref[...]).astype(_BF16)
    qt = _dot_nt(wuqt_ref[...], c_q)
    scale = (MLA_NOPE + MLA_ROPE) ** -0.5 * LOG2_E
    half = MLA_ROPE // 2
    cos_rows = cost_ref[...]
    sin_rows = sint_ref[...]
    zero_rows = jnp.zeros((LANES // 2 - half, qt.shape[1]), _BF16)
    for h in range(MLA_HEADS):
        src = h * (MLA_NOPE + MLA_ROPE)
        dst = h * MLA_QK_PAD
        store_slabs(qt_ref, slice(dst, dst + MLA_NOPE), qt[src:src + MLA_NOPE] * scale)
        x1 = qt[src + MLA_NOPE:src + MLA_NOPE + half]
        x2 = qt[src + MLA_NOPE + half:src + MLA_NOPE + 2 * half]
        dst += MLA_NOPE
        store_slabs(qt_ref, slice(dst, dst + half), (x1 * cos_rows - x2 * sin_rows) * scale)
        store_slabs(qt_ref, slice(dst + half, dst + LANES // 2), zero_rows)
        dst += LANES // 2
        store_slabs(qt_ref, slice(dst, dst + half), (x1 * sin_rows + x2 * cos_rows) * scale)
        store_slabs(qt_ref, slice(dst + half, dst + LANES // 2), zero_rows)


def _spread_rope_columns(w):
    half = MLA_ROPE // 2
    z = jnp.zeros(w.shape[:-1] + (LANES // 2 - half,), w.dtype)
    return jnp.concatenate([w[..., :half], z, w[..., half:], z], axis=-1)


def _projections(h, g_kv, w_kv_a, g_kv_lat, w_uk, w_uv, g_pre, w_in, g_q_lat, w_uq):
    b, s, d = h.shape
    t = b * s
    tb = PROJ_TOKEN_BLOCK
    half = MLA_ROPE // 2
    cos, sin = _rope_angles(s, half)
    ones = np.ones((s, LANES // 2 - half), np.float32)
    zeros = np.zeros((s, LANES // 2 - half), np.float32)
    cos_t = np.concatenate([cos, ones, cos, ones], axis=-1)
    sin_t = np.concatenate([-sin, zeros, sin, zeros], axis=-1)

    wkva = jnp.concatenate([w_kv_a[:, :KV_LORA], _spread_rope_columns(w_kv_a[:, KV_LORA:])], axis=-1)

    const = lambda i: (0, 0)
    tok = lambda i: (i, 0)
    pos = lambda i: (i % (s // tb), 0)
    pos_t = lambda i: (0, i % (s // tb))
    weights = [
        g_kv.reshape(d, 1), wkva, g_kv_lat.reshape(1, KV_LORA), w_uk.astype(_BF16),
        w_uv.T.astype(_BF16), g_pre.reshape(d, 1), w_in, g_q_lat.reshape(1, Q_LORA),
        w_uq.T.astype(_BF16),
    ]

    def token_major(width):
        return jax.ShapeDtypeStruct((t, width), _BF16), pl.BlockSpec((tb, width), tok)

    def feature_major(width):
        return (jax.ShapeDtypeStruct((t // ATTN_BLOCK, width, ATTN_BLOCK), _BF16),
                pl.BlockSpec((tb // ATTN_BLOCK, width, ATTN_BLOCK), lambda i: (i, 0, 0)))

    outs = [feature_major(MLA_HEADS * MLA_QK_PAD), token_major(MLA_HEADS * MLA_NOPE),
            token_major(LANES), feature_major(MLA_WIDTH), token_major(MLA_WIDTH)]
    qt, kn, kr, vt, sg = pl.pallas_call(
        _projection_kernel,
        out_shape=[o[0] for o in outs],
        grid=(t // tb,),
        in_specs=[pl.BlockSpec((tb, d), tok)]
        + [_single_buffered(w.shape, const) for w in weights]
        + [pl.BlockSpec((tb, LANES), pos), pl.BlockSpec((tb, LANES), pos),
           pl.BlockSpec((half, tb), pos_t), pl.BlockSpec((half, tb), pos_t)],
        out_specs=[o[1] for o in outs],
        scratch_shapes=[pltpu.VMEM((d, wkva.shape[1] + w_in.shape[1]), _BF16)],
        compiler_params=pltpu.CompilerParams(
            dimension_semantics=("arbitrary",),
            vmem_limit_bytes=VMEM_LIMIT_BYTES),
        name="latent_projections",
    )(h.reshape(t, d), *weights, cos_t, sin_t,
      np.ascontiguousarray(cos.T), np.ascontiguousarray(sin.T))
    kn, kr, sg = [o.reshape(b, s, -1) for o in (kn, kr, sg)]
    return (qt.reshape(b, s // ATTN_BLOCK, MLA_HEADS * MLA_QK_PAD, ATTN_BLOCK), kn, kr,
            vt.reshape(b, s // ATTN_BLOCK, MLA_WIDTH, ATTN_BLOCK), sg)


def _attention_kernel(qt_ref, kn_ref, kr_ref, vt_ref, sg_ref, h_ref, wout_ref, gpost_ref, o_ref,
                      m_s, den_s, acc_s, y_s):
    blk = ATTN_BLOCK
    qi = pl.program_id(1)
    key_chunk = lax.broadcasted_iota(jnp.int32, (blk, blk), 0) // CHUNK
    query_chunk = lax.broadcasted_iota(jnp.int32, (blk, blk), 1) // CHUNK
    visible = key_chunk <= query_chunk
    ones = jnp.ones((ONES_ROWS, blk), _BF16)

    heads = range(MLA_HEADS)

    def head_scores(ki):
        rows = pl.ds(pl.multiple_of(ki * blk, blk), blk)
        k_rope = kr_ref[0, rows, :]

        def score(h):
            qt = qt_ref[0, 0, h * MLA_QK_PAD:(h + 1) * MLA_QK_PAD, :]
            k = jnp.concatenate([kn_ref[0, rows, h * MLA_NOPE:(h + 1) * MLA_NOPE], k_rope], axis=-1)
            return _dot(k, qt)

        scores = [score(h) for h in heads[:SCORE_LOOKAHEAD]]
        for h in heads:
            if h + SCORE_LOOKAHEAD < MLA_HEADS:
                scores.append(score(h + SCORE_LOOKAHEAD))
            yield h, scores[h]

    def values(ki, h):
        return jnp.concatenate([vt_ref[0, ki, h * MLA_V:(h + 1) * MLA_V, :], ones], axis=0)

    def running_max_step(ki, carry=None):
        seen = jnp.logical_or(visible, ki < qi)
        for h, st in head_scores(ki):
            st = jnp.where(seen, st, -jnp.inf)
            m_prev = m_s[h]
            m_new = jnp.maximum(m_prev, jnp.max(st, axis=0, keepdims=True))
            pt = jnp.exp2(st - m_new).astype(_BF16)
            alpha = jnp.exp2(m_prev - m_new)
            acc_s[h] = alpha * acc_s[h] + _dot(values(ki, h), pt)
            m_s[h] = m_new
        return carry

    def fixed_reference_first_step(last):
        for h, st in head_scores(0):
            if last:
                st = jnp.where(visible, st, -jnp.inf)
            m_ref = jnp.max(st[0:CHUNK], axis=0, keepdims=True)
            m_s[h] = m_ref
            pt = jnp.exp2(st - m_ref).astype(_BF16)
            acc = _dot(values(0, h), pt)
            if last:
                finish_head(h, acc)
            else:
                acc_s[h] = acc

    def finish_head(h, acc):
        head = slice(h * MLA_V, (h + 1) * MLA_V)
        den_s[h] = acc[MLA_V:MLA_V + 1, :]
        o = (acc[0:MLA_V, :] / acc[MLA_V:MLA_V + 1, :]).T
        y_s[:, head] = (sg_ref[0, :, head].astype(_F32) * o).astype(_BF16)

    def finish():
        for h in heads:
            finish_head(h, acc_s[h])

    def fixed_reference_step(ki, last):
        for h, st in head_scores(ki):
            if last:
                st = jnp.where(visible, st, -jnp.inf)
            pt = jnp.exp2(st - m_s[h]).astype(_BF16)
            acc = acc_s[h] + _dot(values(ki, h), pt)
            if last:
                finish_head(h, acc)
            else:
                acc_s[h] = acc

    def reset():
        m_s[...] = jnp.full_like(m_s, -jnp.inf)
        acc_s[...] = jnp.zeros_like(acc_s)

    @pl.when(qi == 0)
    def _():
        fixed_reference_first_step(last=True)

    @pl.when(qi > 0)
    def _():
        fixed_reference_first_step(last=False)
        lax.fori_loop(1, qi, lambda ki, c: (fixed_reference_step(ki, last=False), c)[1], 0)
        fixed_reference_step(qi, last=True)

    in_range = jnp.max(den_s[...]) <= MAX_DENOMINATOR

    def write_output():
        y = _dot(y_s[...], wout_ref[...])
        o_ref[0] = h_ref[0] + y * _rms_scale(y) * gpost_ref[...]

    write_output()

    @pl.when(jnp.logical_not(in_range))
    def _():
        reset()
        lax.fori_loop(0, qi + 1, running_max_step, 0)
        finish()
        write_output()


def _attention_layer(h, qt, kn, kr, vt, sg, w_out, g_post):
    b, s, d = h.shape
    blk = ATTN_BLOCK
    assert vt.shape == (b, s // blk, MLA_WIDTH, blk)
    assert qt.shape == (b, s // blk, MLA_HEADS * MLA_QK_PAD, blk)
    whole = lambda i, j: (i, 0, 0)
    qblk = lambda i, j: (i, j, 0)
    const = lambda i, j: (0, 0)
    return pl.pallas_call(
        _attention_kernel,
        out_shape=jax.ShapeDtypeStruct((b, s, d), _F32),
        grid=(b, s // blk),
        in_specs=[
            pl.BlockSpec((1, 1) + qt.shape[2:], lambda i, j: (i, j, 0, 0)),
            pl.BlockSpec((1, s, kn.shape[-1]), whole),
            pl.BlockSpec((1, s, kr.shape[-1]), whole),
            pl.BlockSpec((1,) + vt.shape[1:], lambda i, j: (i, 0, 0, 0)),
            pl.BlockSpec((1, blk, MLA_WIDTH), qblk),
            pl.BlockSpec((1, blk, d), qblk),
            _single_buffered(w_out.shape, const),
            _single_buffered((1, d), const),
        ],
        out_specs=pl.BlockSpec((1, blk, d), qblk),
        scratch_shapes=[
            pltpu.VMEM((MLA_HEADS, 1, blk), _F32),
            pltpu.VMEM((MLA_HEADS, 1, blk), _F32),
            pltpu.VMEM((MLA_HEADS, MLA_V + ONES_ROWS, blk), _F32),
            pltpu.VMEM((blk, MLA_WIDTH), _BF16),
        ],
        compiler_params=pltpu.CompilerParams(
            dimension_semantics=("parallel", "arbitrary"),
            vmem_limit_bytes=VMEM_LIMIT_BYTES),
        name="latent_attention",
    )(qt, kn, kr, vt, sg, h, w_out.astype(_BF16), g_post.reshape(1, d))


@jax.jit
def kernel(x, g_pre_a, w_in_a, gn_gain_a, w_out_a, g_post_a, g_kv, w_kv_a, g_kv_lat, w_uk, w_uv,
           g_pre_b, w_in_b, g_q_lat, w_uq, w_out_b, g_post_b):
    assert w_in_a.shape[0] == 1 and w_in_b.shape[0] == 1
    h = _retention_layer(x, g_pre_a[0], w_in_a[0], gn_gain_a[0], w_out_a[0], g_post_a[0])
    qt, kn, kr, vt, sg = _projections(h, g_kv, w_kv_a, g_kv_lat, w_uk, w_uv,
                                      g_pre_b[0], w_in_b[0], g_q_lat[0], w_uq[0])
    return _attention_layer(h, qt, kn, kr, vt, sg, w_out_b[0], g_post_b[0])
```
